```python
import math
import jax, jax.numpy as jnp
from jax import lax
import numpy as np

D_MODEL = 4096
BATCH = 4
SEQ = 4096
DEPTH = 2
DEC_BATCH = 8
DEC_SEQ = 2048
PAST_LEN = 128

N_MIXERS = 2
N_HEADS = 16
HEAD_DIM = 128
V_DIM = 2 * HEAD_DIM
ROT_DIM = HEAD_DIM // 4
ROPE_THETA = 500000.0
Q_BLOCK = 128
N_FOURIER_GROUPS = 8
FOURIER_GROUP_DIM = D_MODEL // N_FOURIER_GROUPS
D_FF = ((8 * D_MODEL // 3 + 255) // 256) * 256
N_ATTN_LAYERS = (DEPTH + N_MIXERS - 1) // N_MIXERS
N_FOURIER_LAYERS = DEPTH // N_MIXERS
EPS = 1e-6
SUBLN_EPS = 1e-5
LAMBDA_STD = 0.1

kernel_name = "hybrid_diffattn_fnet_encoder"


def rmsnorm(x, g, eps=EPS):
    xf = x.astype(jnp.float32)
    y = xf * lax.rsqrt(jnp.mean(xf * xf, axis=-1, keepdims=True) + eps) * g.astype(jnp.float32)
    return y.astype(x.dtype)


def rope_tables(seq_len):
    inv_freq = ROPE_THETA ** (-jnp.arange(0, ROT_DIM, 2, dtype=jnp.float32) / ROT_DIM)
    ang = jnp.arange(seq_len, dtype=jnp.float32)[:, None] * inv_freq[None, :]
    return jnp.cos(ang)[:, None, None, :], jnp.sin(ang)[:, None, None, :]


def apply_partial_rope(t, cos, sin):
    rot = t[..., :ROT_DIM].astype(jnp.float32)
    t1, t2 = rot[..., :ROT_DIM // 2], rot[..., ROT_DIM // 2:]
    r = jnp.concatenate([t1 * cos - t2 * sin, t2 * cos + t1 * sin], axis=-1)
    return jnp.concatenate([r.astype(t.dtype), t[..., ROT_DIM:]], axis=-1)


def diff_attention(h, w_qkv, w_o, lq1, lk1, lq2, lk2, subln, lambda_init):
    B, S, _ = h.shape
    qkv = h @ w_qkv
    q, k, v = jnp.split(qkv, 3, axis=-1)
    q = q.reshape(B, S, N_HEADS, 2, HEAD_DIM)
    k = k.reshape(B, S, N_HEADS, 2, HEAD_DIM)
    v = v.reshape(B, S, N_HEADS, V_DIM)
    cos, sin = rope_tables(S)
    q = apply_partial_rope(q, cos, sin) * (HEAD_DIM ** -0.5)
    k = apply_partial_rope(k, cos, sin)
    lam = (jnp.exp(jnp.sum(lq1.astype(jnp.float32) * lk1.astype(jnp.float32)))
           - jnp.exp(jnp.sum(lq2.astype(jnp.float32) * lk2.astype(jnp.float32)))
           + lambda_init)
    n_blk = S // Q_BLOCK
    q_blocks = q.reshape(B, n_blk, Q_BLOCK, N_HEADS, 2, HEAD_DIM).transpose(1, 0, 2, 3, 4, 5)

    def attend(qb):
        s = jnp.einsum('bqhcd,bkhcd->bhcqk', qb, k, preferred_element_type=jnp.float32)
        p = jax.nn.softmax(s, axis=-1)
        a = p[:, :, 0] - lam * p[:, :, 1]
        return jnp.einsum('bhqk,bkhe->bqhe', a.astype(v.dtype), v)

    o = lax.map(attend, q_blocks)
    o = o.transpose(1, 0, 2, 3, 4).reshape(B, S, N_HEADS, V_DIM)
    o = rmsnorm(o, subln, SUBLN_EPS) * (1.0 - lambda_init)
    return o.reshape(B, S, D_MODEL) @ w_o


def fourier_mix(h, w_f):
    B, S, _ = h.shape
    hg = h.astype(jnp.float32).reshape(B, S, N_FOURIER_GROUPS, FOURIER_GROUP_DIM)
    f = jnp.fft.fft2(hg, axes=(1, 3), norm='ortho').real
    return f.astype(h.dtype).reshape(B, S, D_MODEL) @ w_f


def swiglu(h, w_gate_up, w_down):
    g, u = jnp.split(h @ w_gate_up, 2, axis=-1)
    return (jax.nn.silu(g) * u) @ w_down


def trunk(x, norm_mix, norm_ffn, norm_final, attn_w_qkv, attn_w_o, attn_lambda_q1, attn_lambda_k1,
          attn_lambda_q2, attn_lambda_k2, attn_subln, fourier_w_o, ffn_w_gate_up, ffn_w_down):
    for i in range(DEPTH):
        h = rmsnorm(x, norm_mix[i])
        j = i // N_MIXERS
        if i % N_MIXERS == 0:
            lambda_init = 0.8 - 0.6 * math.exp(-0.3 * i)
            x = x + diff_attention(h, attn_w_qkv[j], attn_w_o[j], attn_lambda_q1[j], attn_lambda_k1[j],
                                   attn_lambda_q2[j], attn_lambda_k2[j], attn_subln[j], lambda_init)
        else:
            x = x + fourier_mix(h, fourier_w_o[j])
        x = x + swiglu(rmsnorm(x, norm_ffn[i]), ffn_w_gate_up[i], ffn_w_down[i])
    return rmsnorm(x, norm_final)


def setup_inputs(seed: int = 0) -> dict:
    key = jax.random.key(seed)
    ks = jax.random.split(key, 16)
    f32 = jnp.float32
    d_sc = D_MODEL ** -0.5
    return {
        'x_prompt': jax.random.normal(ks[0], (BATCH, SEQ, D_MODEL), f32),
        'x_sample': jax.random.normal(ks[1], (DEC_BATCH, DEC_SEQ, D_MODEL), f32),
        'norm_mix': 1.0 + 0.02 * jax.random.normal(ks[2], (DEPTH, D_MODEL), f32),
        'norm_ffn': 1.0 + 0.02 * jax.random.normal(ks[3], (DEPTH, D_MODEL), f32),
        'norm_final': 1.0 + 0.02 * jax.random.normal(ks[4], (D_MODEL,), f32),
        'attn_w_qkv': jax.random.normal(ks[5], (N_ATTN_LAYERS, D_MODEL, 3 * D_MODEL), f32) * d_sc,
        'attn_w_o': jax.random.normal(ks[6], (N_ATTN_LAYERS, D_MODEL, D_MODEL), f32) * d_sc,
        'attn_lambda_q1': LAMBDA_STD * jax.random.normal(ks[7], (N_ATTN_LAYERS, HEAD_DIM), f32),
        'attn_lambda_k1': LAMBDA_STD * jax.random.normal(ks[8], (N_ATTN_LAYERS, HEAD_DIM), f32),
        'attn_lambda_q2': LAMBDA_STD * jax.random.normal(ks[9], (N_ATTN_LAYERS, HEAD_DIM), f32),
        'attn_lambda_k2': LAMBDA_STD * jax.random.normal(ks[10], (N_ATTN_LAYERS, HEAD_DIM), f32),
        'attn_subln': 1.0 + 0.02 * jax.random.normal(ks[11], (N_ATTN_LAYERS, V_DIM), f32),
        'fourier_w_o': jax.random.normal(ks[12], (N_FOURIER_LAYERS, D_MODEL, D_MODEL), f32) * d_sc,
        'ffn_w_gate_up': jax.random.normal(ks[13], (DEPTH, D_MODEL, 2 * D_FF), f32) * d_sc,
        'ffn_w_down': jax.random.normal(ks[14], (DEPTH, D_FF, D_MODEL), f32) * (D_FF ** -0.5),
    }


def reference(x_prompt, x_sample, norm_mix, norm_ffn, norm_final, attn_w_qkv, attn_w_o, attn_lambda_q1,
              attn_lambda_k1, attn_lambda_q2, attn_lambda_k2, attn_subln, fourier_w_o, ffn_w_gate_up, ffn_w_down):
    y_prompt = trunk(x_prompt, norm_mix, norm_ffn, norm_final, attn_w_qkv, attn_w_o, attn_lambda_q1,
                     attn_lambda_k1, attn_lambda_q2, attn_lambda_k2, attn_subln, fourier_w_o,
                     ffn_w_gate_up, ffn_w_down)
    y_sample = trunk(x_sample, norm_mix, norm_ffn, norm_final, attn_w_qkv, attn_w_o, attn_lambda_q1,
                     attn_lambda_k1, attn_lambda_q2, attn_lambda_k2, attn_subln, fourier_w_o,
                     ffn_w_gate_up, ffn_w_down)
    return (y_prompt, y_sample)
```

```python
import functools
import math

import numpy as np
import jax
import jax.numpy as jnp
from jax import lax
from jax.experimental import pallas as pl
from jax.experimental.pallas import tpu as pltpu

HEAD_DIM = 128
V_DIM = 2 * HEAD_DIM
ROT_DIM = HEAD_DIM // 4
ROT_HALF = ROT_DIM // 2
ROPE_THETA = 500000.0
N_FOURIER_GROUPS = 8
N_MIXERS = 2
EPS = 1e-6
SUBLN_EPS = 1e-5
LANES = 128
VMEM_LIMIT_BYTES = 58 * 1024 * 1024

BF16 = jnp.bfloat16
F32 = jnp.float32


def _block(n, want):
    b = min(n, want)
    while n % b:
        b //= 2
    return b


def _params(*sem):
    return pltpu.CompilerParams(dimension_semantics=sem, vmem_limit_bytes=VMEM_LIMIT_BYTES)


def _rmsnorm_rows(x, g, eps):
    return x * lax.rsqrt(jnp.mean(x * x, axis=-1, keepdims=True) + eps) * g


NORM_ROWS = 16


def _rmsnorm_block(x_ref, g_ref, o_ref, eps):
    g = g_ref[...]

    def body(r, carry):
        rows = pl.ds(pl.multiple_of(r * NORM_ROWS, NORM_ROWS), NORM_ROWS)
        o_ref[rows, :] = _rmsnorm_rows(x_ref[rows, :], g, eps).astype(o_ref.dtype)
        return carry

    lax.fori_loop(0, x_ref.shape[0] // NORM_ROWS, body, 0)


def _qkv_kernel(x_ref, g_ref, w_ref, cos_ref, sa_ref, sb_ref, o_ref, h_ref, *, n_q_blocks, q_scale):
    j = pl.program_id(1)

    @pl.when(j == 0)
    def _():
        _rmsnorm_block(x_ref, g_ref, h_ref, EPS)

    def project():
        return jnp.dot(h_ref[...], w_ref[...], preferred_element_type=F32)

    def rotary(scale):
        acc = project()
        cos, sa, sb = cos_ref[...], sa_ref[...], sb_ref[...]
        for c in range(acc.shape[1] // HEAD_DIM):
            t = acc[:, c * HEAD_DIM:(c + 1) * HEAD_DIM]
            r = (t * cos + pltpu.roll(t, HEAD_DIM - ROT_HALF, 1) * sa + pltpu.roll(t, ROT_HALF, 1) * sb)
            o_ref[:, c * HEAD_DIM:(c + 1) * HEAD_DIM] = (r * scale).astype(o_ref.dtype)

    @pl.when(j < n_q_blocks)
    def _():
        rotary(q_scale)

    @pl.when(jnp.logical_and(j >= n_q_blocks, j < 2 * n_q_blocks))
    def _():
        rotary(1.0)

    @pl.when(j >= 2 * n_q_blocks)
    def _():
        o_ref[...] = project().astype(o_ref.dtype)


def _qkv_proj(x2d, gain, w_qkv, rope, seq):
    m, d = x2d.shape
    n = w_qkv.shape[1]
    bm = _block(seq, 512)
    bn = _block(d, 512)
    n_pos_blocks = seq // bm
    cos, sa, sb = rope
    rope_spec = pl.BlockSpec((bm, HEAD_DIM), lambda i, j: (i % n_pos_blocks, 0))
    return pl.pallas_call(
        functools.partial(_qkv_kernel, n_q_blocks=d // bn, q_scale=HEAD_DIM ** -0.5),
        grid=(m // bm, n // bn),
        in_specs=[
            pl.BlockSpec((bm, d), lambda i, j: (i, 0)),
            pl.BlockSpec((1, d), lambda i, j: (0, 0)),
            pl.BlockSpec((d, bn), lambda i, j: (0, j)),
            rope_spec, rope_spec, rope_spec,
        ],
        out_specs=pl.BlockSpec((bm, bn), lambda i, j: (i, j)),
        out_shape=jax.ShapeDtypeStruct((m, n), BF16),
        scratch_shapes=[pltpu.VMEM((bm, d), BF16)],
        compiler_params=_params("parallel", "arbitrary"),
        name="qkv",
    )(x2d, gain, w_qkv, cos, sa, sb)


def _rope_tables(seq):
    inv_freq = ROPE_THETA ** (-jnp.arange(0, ROT_DIM, 2, dtype=F32) / ROT_DIM)
    ang = jnp.arange(seq, dtype=F32)[:, None] * inv_freq[None, :]
    c, s = jnp.cos(ang), jnp.sin(ang)
    pad = HEAD_DIM - ROT_DIM
    cos = jnp.concatenate([c, c, jnp.ones((seq, pad), F32)], axis=1)
    sa = jnp.concatenate([-s, jnp.zeros((seq, pad + ROT_HALF), F32)], axis=1)
    sb = jnp.concatenate([jnp.zeros((seq, ROT_HALF), F32), s, jnp.zeros((seq, pad), F32)], axis=1)
    return cos, sa, sb


def _attn_kernel(q_ref, k_ref, v_ref, lq1_ref, lk1_ref, lq2_ref, lk2_ref, sub_ref, o_ref, *, lambda_init):
    lam = (jnp.exp(jnp.sum(lq1_ref[...] * lk1_ref[...], axis=-1, keepdims=True))
           - jnp.exp(jnp.sum(lq2_ref[...] * lk2_ref[...], axis=-1, keepdims=True))
           + lambda_init)
    q = q_ref[...]
    k = k_ref[...]
    nt = (((1,), (1,)), ((), ()))

    def softmax_parts(c):
        s = lax.dot_general(q[:, c * HEAD_DIM:(c + 1) * HEAD_DIM], k[:, c * HEAD_DIM:(c + 1) * HEAD_DIM], nt,
                            preferred_element_type=F32)
        p = jnp.exp(s - jnp.max(s, axis=-1, keepdims=True))
        return p, 1.0 / jnp.sum(p, axis=-1, keepdims=True)

    p1, r1 = softmax_parts(0)
    p2, r2 = softmax_parts(1)
    a = p1 * r1 - p2 * (r2 * lam)
    o = jnp.dot(a.astype(BF16), v_ref[...], preferred_element_type=F32)
    o = _rmsnorm_rows(o, sub_ref[...], SUBLN_EPS) * (1.0 - lambda_init)
    o_ref[...] = o.astype(o_ref.dtype)


def _diff_attention(qkv, lq1, lk1, lq2, lk2, subln, batch, seq, lambda_init):
    m, n3 = qkv.shape
    d = n3 // 3
    heads = d // V_DIM
    bq = _block(seq, 256)
    nq = seq // bq
    vec = pl.BlockSpec((1, HEAD_DIM), lambda b, h, i: (0, 0))
    return pl.pallas_call(
        functools.partial(_attn_kernel, lambda_init=lambda_init),
        grid=(batch, heads, nq),
        in_specs=[
            pl.BlockSpec((bq, V_DIM), lambda b, h, i: (b * nq + i, h)),
            pl.BlockSpec((seq, V_DIM), lambda b, h, i: (b, heads + h)),
            pl.BlockSpec((seq, V_DIM), lambda b, h, i: (b, 2 * heads + h)),
            vec, vec, vec, vec,
            pl.BlockSpec((1, V_DIM), lambda b, h, i: (0, 0)),
        ],
        out_specs=pl.BlockSpec((bq, V_DIM), lambda b, h, i: (b * nq + i, h)),
        out_shape=jax.ShapeDtypeStruct((m, d), BF16),
        compiler_params=_params("parallel", "parallel", "arbitrary"),
        name="diff_attn",
    )(qkv, qkv, qkv, lq1, lk1, lq2, lk2, subln)


def _proj_kernel(a_ref, w_ref, res_ref, o_ref):
    o_ref[...] = res_ref[...] + jnp.dot(a_ref[...], w_ref[...], preferred_element_type=F32)


def _proj_residual(a, w, res):
    m, k = a.shape
    n = w.shape[1]
    bm = _block(m, 1024)
    bn = _block(n, 512)
    return pl.pallas_call(
        _proj_kernel,
        grid=(m // bm, n // bn),
        in_specs=[
            pl.BlockSpec((bm, k), lambda i, j: (i, 0)),
            pl.BlockSpec((k, bn), lambda i, j: (0, j)),
            pl.BlockSpec((bm, bn), lambda i, j: (i, j)),
        ],
        out_specs=pl.BlockSpec((bm, bn), lambda i, j: (i, j)),
        out_shape=jax.ShapeDtypeStruct((m, n), F32),
        compiler_params=_params("parallel", "arbitrary"),
        name="proj_residual",
    )(a, w, res)


def _ffn_kernel(x_ref, g_ref, wg_ref, wu_ref, wd_ref, gf_ref, o_ref, h_ref, *, final_norm):
    j = pl.program_id(1)

    @pl.when(j == 0)
    def _():
        _rmsnorm_block(x_ref, g_ref, h_ref, EPS)
        o_ref[...] = x_ref[...]

    h = h_ref[...]
    gate = jnp.dot(h, wg_ref[...], preferred_element_type=F32)
    up = jnp.dot(h, wu_ref[...], preferred_element_type=F32)
    act = (gate / (1.0 + jnp.exp(-gate)) * up).astype(BF16)
    o_ref[...] += jnp.dot(act, wd_ref[...], preferred_element_type=F32)

    if final_norm:
        @pl.when(j == pl.num_programs(1) - 1)
        def _():
            _rmsnorm_block(o_ref, gf_ref, o_ref, EPS)


def _ffn(x2d, gain, w_gate_up, w_down, gain_final, final_norm):
    m, d = x2d.shape
    f = w_down.shape[0]
    bm = _block(m, 512)
    bc = _block(f, 256)
    nf = f // bc
    return pl.pallas_call(
        functools.partial(_ffn_kernel, final_norm=final_norm),
        grid=(m // bm, nf),
        in_specs=[
            pl.BlockSpec((bm, d), lambda i, j: (i, 0)),
            pl.BlockSpec((1, d), lambda i, j: (0, 0)),
            pl.BlockSpec((d, bc), lambda i, j: (0, j)),
            pl.BlockSpec((d, bc), lambda i, j: (0, nf + j)),
            pl.BlockSpec((bc, d), lambda i, j: (j, 0)),
            pl.BlockSpec((1, d), lambda i, j: (0, 0)),
        ],
        out_specs=pl.BlockSpec((bm, d), lambda i, j: (i, 0)),
        out_shape=jax.ShapeDtypeStruct((m, d), F32),
        scratch_shapes=[pltpu.VMEM((bm, d), BF16)],
        compiler_params=_params("parallel", "arbitrary"),
        name="ffn",
    )(x2d, gain, w_gate_up, w_gate_up, w_down, gain_final)


def _dft_matrices(n, scale):
    r = 1
    while r * r < n:
        r *= 2
    k = jnp.arange(n, dtype=jnp.int32)[:, None]
    hi = jnp.arange(n // r, dtype=jnp.int32)[None, :] * r
    lo = jnp.arange(r, dtype=jnp.int32)[None, :]
    w = 2.0 * math.pi / n
    a_hi = ((k * hi) % n).astype(F32) * w
    a_lo = ((k * lo) % n).astype(F32) * w
    ch, sh = jnp.cos(a_hi)[:, :, None], jnp.sin(a_hi)[:, :, None]
    cl, sl = jnp.cos(a_lo)[:, None, :], jnp.sin(a_lo)[:, None, :]
    c = (ch * cl - sh * sl).reshape(n, n) * scale
    s = (sh * cl + ch * sl).reshape(n, n) * scale
    return c.astype(BF16), s.astype(BF16)


def _dft_ch_kernel(x_ref, g_ref, cs_ref, hc_ref, hs_ref, h_ref, *, group_dim):
    _rmsnorm_block(x_ref, g_ref, h_ref, EPS)
    h = h_ref[...]
    cs = cs_ref[...]
    for g in range(h.shape[1] // group_dim):
        y = jnp.dot(h[:, g * group_dim:(g + 1) * group_dim], cs, preferred_element_type=F32)
        hc_ref[:, g * group_dim:(g + 1) * group_dim] = y[:, :group_dim].astype(hc_ref.dtype)
        hs_ref[:, g * group_dim:(g + 1) * group_dim] = y[:, group_dim:].astype(hs_ref.dtype)


def _dft_channels(x2d, gain, cs):
    m, d = x2d.shape
    gd = cs.shape[0]
    bm = _block(m, 512)
    row = pl.BlockSpec((bm, d), lambda i: (i, 0))
    return pl.pallas_call(
        functools.partial(_dft_ch_kernel, group_dim=gd),
        grid=(m // bm,),
        in_specs=[row, pl.BlockSpec((1, d), lambda i: (0, 0)), pl.BlockSpec(cs.shape, lambda i: (0, 0))],
        out_specs=[row, row],
        out_shape=[jax.ShapeDtypeStruct((m, d), BF16)] * 2,
        scratch_shapes=[pltpu.VMEM((bm, d), BF16)],
        compiler_params=_params("parallel"),
        name="dft_channels",
    )(x2d, gain, cs)


def _dft_pos_kernel(c_ref, sn_ref, hc_ref, hs_ref, o_ref):
    y = (jnp.dot(c_ref[...], hc_ref[...], preferred_element_type=F32)
         + jnp.dot(sn_ref[...], hs_ref[...], preferred_element_type=F32))
    o_ref[...] = y.astype(o_ref.dtype)


def _dft_positions(cmat, snmat, hc, hs, batch, seq):
    m, d = hc.shape
    bm = _block(seq, 512)
    bn = _block(d, 512)
    ni = seq // bm
    mat = pl.BlockSpec((bm, seq), lambda b, i, j: (i, 0))
    col = pl.BlockSpec((seq, bn), lambda b, i, j: (b, j))
    return pl.pallas_call(
        _dft_pos_kernel,
        grid=(batch, ni, d // bn),
        in_specs=[mat, mat, col, col],
        out_specs=pl.BlockSpec((bm, bn), lambda b, i, j: (b * ni + i, j)),
        out_shape=jax.ShapeDtypeStruct((m, d), BF16),
        compiler_params=_params("parallel", "parallel", "arbitrary"),
        name="dft_positions",
    )(cmat, snmat, hc, hs)


def _trunk(x, p):
    batch, seq, d = x.shape
    depth = p["norm_mix"].shape[0]
    x2d = x.reshape(batch * seq, d)
    rope = _rope_tables(seq)
    group_dim = d // N_FOURIER_GROUPS
    for i in range(depth):
        j = i // N_MIXERS
        gain_mix = p["norm_mix"][i][None, :]
        if i % N_MIXERS == 0:
            lambda_init = 0.8 - 0.6 * math.exp(-0.3 * i)
            qkv = _qkv_proj(x2d, gain_mix, p["w_qkv"][j], rope, seq)
            o = _diff_attention(qkv, p["lq1"][j][None, :], p["lk1"][j][None, :], p["lq2"][j][None, :],
                                p["lk2"][j][None, :], p["subln"][j][None, :], batch, seq, lambda_init)
            x2d = _proj_residual(o, p["w_o"][j], x2d)
        else:
            cc, sc = _dft_matrices(group_dim, group_dim ** -0.5)
            cp, sp = _dft_matrices(seq, seq ** -0.5)
            hc, hs = _dft_channels(x2d, gain_mix, jnp.concatenate([cc, sc], axis=1))
            f = _dft_positions(cp, -sp, hc, hs, batch, seq)
            x2d = _proj_residual(f, p["w_f"][j], x2d)
        last = i == depth - 1
        x2d = _ffn(x2d, p["norm_ffn"][i][None, :], p["w_gate_up"][i], p["w_down"][i],
                   p["norm_final"][None, :], final_norm=last)
    return x2d.reshape(batch, seq, d)


def kernel(x_prompt, x_sample, norm_mix, norm_ffn, norm_final, attn_w_qkv, attn_w_o, attn_lambda_q1,
           attn_lambda_k1, attn_lambda_q2, attn_lambda_k2, attn_subln, fourier_w_o, ffn_w_gate_up, ffn_w_down):
    assert x_prompt.shape[-1] % V_DIM == 0 and x_prompt.shape[-1] % (N_FOURIER_GROUPS * LANES) == 0
    p = {
        "norm_mix": norm_mix, "norm_ffn": norm_ffn, "norm_final": norm_final,
        "w_qkv": attn_w_qkv.astype(BF16), "w_o": attn_w_o.astype(BF16),
        "lq1": attn_lambda_q1, "lk1": attn_lambda_k1, "lq2": attn_lambda_q2, "lk2": attn_lambda_k2,
        "subln": attn_subln, "w_f": fourier_w_o.astype(BF16),
        "w_gate_up": ffn_w_gate_up.astype(BF16), "w_down": ffn_w_down.astype(BF16),
    }
    return (_trunk(x_prompt, p), _trunk(x_sample, p))
```

```python
import functools
import math

import numpy as np
import jax
import jax.numpy as jnp
from jax import lax
from jax.experimental import pallas as pl
from jax.experimental.pallas import tpu as pltpu

HEAD_DIM = 128
V_DIM = 2 * HEAD_DIM
ROT_DIM = HEAD_DIM // 4
ROT_HALF = ROT_DIM // 2
ROPE_THETA = 500000.0
N_FOURIER_GROUPS = 8
N_MIXERS = 2
EPS = 1e-6
SUBLN_EPS = 1e-5
LANES = 128
VMEM_LIMIT_BYTES = 58 * 1024 * 1024

BF16 = jnp.bfloat16
F32 = jnp.float32


def _block(n, want):
    b = min(n, want)
    while n % b:
        b //= 2
    return b


def _params(*sem):
    return pltpu.CompilerParams(dimension_semantics=sem, vmem_limit_bytes=VMEM_LIMIT_BYTES)


def _rmsnorm_rows(x, g, eps):
    return x * lax.rsqrt(jnp.mean(x * x, axis=-1, keepdims=True) + eps) * g


NORM_ROWS = 16
NORM_UNROLL = 4


def _rmsnorm_block(x_ref, g_ref, o_ref, eps):
    g = g_ref[...]

    def body(r, carry):
        rows = pl.ds(pl.multiple_of(r * NORM_ROWS, NORM_ROWS), NORM_ROWS)
        o_ref[rows, :] = _rmsnorm_rows(x_ref[rows, :], g, eps).astype(o_ref.dtype)
        return carry

    lax.fori_loop(0, x_ref.shape[0] // NORM_ROWS, body, 0, unroll=NORM_UNROLL)


ROPE_COLS = 512


def _qkv_kernel(x_ref, g_ref, w_ref, cos_ref, sa_ref, sb_ref, o_ref, h_ref, *, n_q_blocks, q_scale):
    j = pl.program_id(1)

    @pl.when(j == 0)
    def _():
        _rmsnorm_block(x_ref, g_ref, h_ref, EPS)

    def project(lo=0, width=None):
        width = w_ref.shape[1] if width is None else width
        return jnp.dot(h_ref[...], w_ref[:, lo:lo + width], preferred_element_type=F32)

    def rotary(scale):
        cos, sa, sb = cos_ref[...] * scale, sa_ref[...] * scale, sb_ref[...] * scale
        width = min(ROPE_COLS, w_ref.shape[1])
        for lo in range(0, w_ref.shape[1], width):
            acc = project(lo, width)
            for c in range(lo, lo + width, HEAD_DIM):
                t = acc[:, c - lo:c - lo + HEAD_DIM]
                r = t * cos + pltpu.roll(t, HEAD_DIM - ROT_HALF, 1) * sa + pltpu.roll(t, ROT_HALF, 1) * sb
                o_ref[:, c:c + HEAD_DIM] = r.astype(o_ref.dtype)

    @pl.when(j < n_q_blocks)
    def _():
        rotary(q_scale)

    @pl.when(jnp.logical_and(j >= n_q_blocks, j < 2 * n_q_blocks))
    def _():
        rotary(1.0)

    @pl.when(j >= 2 * n_q_blocks)
    def _():
        o_ref[...] = project().astype(o_ref.dtype)


def _qkv_proj(x2d, gain, w_qkv, rope, seq):
    m, d = x2d.shape
    n = w_qkv.shape[1]
    bm = _block(seq, 512)
    bn = _block(d, 1024)
    n_pos_blocks = seq // bm
    cos, sa, sb = rope
    rope_spec = pl.BlockSpec((bm, HEAD_DIM), lambda i, j: (i % n_pos_blocks, 0))
    return pl.pallas_call(
        functools.partial(_qkv_kernel, n_q_blocks=d // bn, q_scale=HEAD_DIM ** -0.5 * math.log2(math.e)),
        grid=(m // bm, n // bn),
        in_specs=[
            pl.BlockSpec((bm, d), lambda i, j: (i, 0)),
            pl.BlockSpec((1, d), lambda i, j: (0, 0)),
            pl.BlockSpec((d, bn), lambda i, j: (0, j)),
            rope_spec, rope_spec, rope_spec,
        ],
        out_specs=pl.BlockSpec((bm, bn), lambda i, j: (i, j)),
        out_shape=jax.ShapeDtypeStruct((m, n), BF16),
        scratch_shapes=[pltpu.VMEM((bm, d), BF16)],
        compiler_params=_params("parallel", "arbitrary"),
        name="qkv",
    )(x2d, gain, w_qkv, cos, sa, sb)


def _rope_tables(seq):
    inv_freq = ROPE_THETA ** (-jnp.arange(0, ROT_DIM, 2, dtype=F32) / ROT_DIM)
    ang = jnp.arange(seq, dtype=F32)[:, None] * inv_freq[None, :]
    c, s = jnp.cos(ang), jnp.sin(ang)
    pad = HEAD_DIM - ROT_DIM
    cos = jnp.concatenate([c, c, jnp.ones((seq, pad), F32)], axis=1)
    sa = jnp.concatenate([-s, jnp.zeros((seq, pad + ROT_HALF), F32)], axis=1)
    sb = jnp.concatenate([jnp.zeros((seq, ROT_HALF), F32), s, jnp.zeros((seq, pad), F32)], axis=1)
    return cos, sa, sb


ATTN_ROWS = 128


def _attn_kernel(q_ref, k_ref, v_ref, lq1_ref, lk1_ref, lq2_ref, lk2_ref, sub_ref, o_ref, *, lambda_init):
    lam = (jnp.exp(jnp.sum(lq1_ref[...] * lk1_ref[...], axis=-1, keepdims=True))
           - jnp.exp(jnp.sum(lq2_ref[...] * lk2_ref[...], axis=-1, keepdims=True))
           + lambda_init)
    k = k_ref[...]
    v = v_ref[...]
    nt = (((1,), (1,)), ((), ()))
    rows = min(ATTN_ROWS, q_ref.shape[0])
    tasks = [(r, c) for r in range(q_ref.shape[0] // rows) for c in range(2)]

    def scores(t):
        r, c = tasks[t]
        return lax.dot_general(q_ref[r * rows:(r + 1) * rows, c * HEAD_DIM:(c + 1) * HEAD_DIM],
                               k[:, c * HEAD_DIM:(c + 1) * HEAD_DIM], nt, preferred_element_type=F32)

    def exponentials(s):
        p = jnp.exp2(s - jnp.max(s, axis=-1, keepdims=True))
        return p.astype(BF16), jnp.sum(p, axis=-1, keepdims=True)

    n = len(tasks)
    s = {t: scores(t) for t in range(min(2, n))}
    p = {0: exponentials(s.pop(0))}
    pv, norm = {}, {}
    for t in range(n):
        pb, norm[t] = p.pop(t)
        pv[t] = jnp.dot(pb, v, preferred_element_type=F32)
        if t + 2 < n:
            s[t + 2] = scores(t + 2)
        if t + 1 < n:
            p[t + 1] = exponentials(s.pop(t + 1))
        if tasks[t][1] == 1:
            r = tasks[t][0]
            o = pv.pop(t - 1) * (1.0 / norm[t - 1]) - pv.pop(t) * (lam / norm[t])
            o = _rmsnorm_rows(o, sub_ref[...], SUBLN_EPS) * (1.0 - lambda_init)
            o_ref[r * rows:(r + 1) * rows, :] = o.astype(o_ref.dtype)


def _diff_attention(qkv, lq1, lk1, lq2, lk2, subln, batch, seq, lambda_init):
    m, n3 = qkv.shape
    d = n3 // 3
    heads = d // V_DIM
    bq = _block(seq, 512)
    nq = seq // bq
    vec = pl.BlockSpec((1, HEAD_DIM), lambda b, h, i: (0, 0))
    return pl.pallas_call(
        functools.partial(_attn_kernel, lambda_init=lambda_init),
        grid=(batch, heads, nq),
        in_specs=[
            pl.BlockSpec((bq, V_DIM), lambda b, h, i: (b * nq + i, h)),
            pl.BlockSpec((seq, V_DIM), lambda b, h, i: (b, heads + h)),
            pl.BlockSpec((seq, V_DIM), lambda b, h, i: (b, 2 * heads + h)),
            vec, vec, vec, vec,
            pl.BlockSpec((1, V_DIM), lambda b, h, i: (0, 0)),
        ],
        out_specs=pl.BlockSpec((bq, V_DIM), lambda b, h, i: (b * nq + i, h)),
        out_shape=jax.ShapeDtypeStruct((m, d), BF16),
        compiler_params=_params("parallel", "parallel", "arbitrary"),
        name="diff_attn",
    )(qkv, qkv, qkv, lq1, lk1, lq2, lk2, subln)


def _proj_kernel(a_ref, w_ref, res_ref, o_ref):
    o_ref[...] = res_ref[...] + jnp.dot(a_ref[...], w_ref[...], preferred_element_type=F32)


def _proj_residual(a, w, res):
    m, k = a.shape
    n = w.shape[1]
    bm = _block(m, 1024)
    bn = _block(n, 512)
    return pl.pallas_call(
        _proj_kernel,
        grid=(m // bm, n // bn),
        in_specs=[
            pl.BlockSpec((bm, k), lambda i, j: (i, 0)),
            pl.BlockSpec((k, bn), lambda i, j: (0, j)),
            pl.BlockSpec((bm, bn), lambda i, j: (i, j)),
        ],
        out_specs=pl.BlockSpec((bm, bn), lambda i, j: (i, j)),
        out_shape=jax.ShapeDtypeStruct((m, n), F32),
        compiler_params=_params("parallel", "arbitrary"),
        name="proj_residual",
    )(a, w, res)


def _ffn_kernel(x_hbm, g_ref, wg_ref, wu_ref, wd_ref, gf_ref, o_ref, h_ref, sem, *, final_norm):
    i, j = pl.program_id(0), pl.program_id(1)
    bm = o_ref.shape[0]

    @pl.when(j == 0)
    def _():
        rows = pl.ds(pl.multiple_of(i * bm, bm), bm)
        fetch = pltpu.make_async_copy(x_hbm.at[rows, :], o_ref, sem)
        fetch.start()
        fetch.wait()
        _rmsnorm_block(o_ref, g_ref, h_ref, EPS)

    h = h_ref[...]
    gate = jnp.dot(h, wg_ref[...], preferred_element_type=F32)
    up = jnp.dot(h, wu_ref[...], preferred_element_type=F32)
    act = (gate / (1.0 + jnp.exp(-gate)) * up).astype(BF16)
    o_ref[...] += jnp.dot(act, wd_ref[...], preferred_element_type=F32)

    if final_norm:
        @pl.when(j == pl.num_programs(1) - 1)
        def _():
            _rmsnorm_block(o_ref, gf_ref, o_ref, EPS)


def _ffn(x2d, gain, w_gate_up, w_down, gain_final, final_norm):
    m, d = x2d.shape
    f = w_down.shape[0]
    bm = _block(m, 1024)
    bc = _block(f, 256)
    nf = f // bc
    return pl.pallas_call(
        functools.partial(_ffn_kernel, final_norm=final_norm),
        grid=(m // bm, nf),
        in_specs=[
            pl.BlockSpec(memory_space=pl.ANY),
            pl.BlockSpec((1, d), lambda i, j: (0, 0)),
            pl.BlockSpec((d, bc), lambda i, j: (0, j)),
            pl.BlockSpec((d, bc), lambda i, j: (0, nf + j)),
            pl.BlockSpec((bc, d), lambda i, j: (j, 0)),
            pl.BlockSpec((1, d), lambda i, j: (0, 0)),
        ],
        out_specs=pl.BlockSpec((bm, d), lambda i, j: (i, 0)),
        out_shape=jax.ShapeDtypeStruct((m, d), F32),
        scratch_shapes=[pltpu.VMEM((bm, d), BF16), pltpu.SemaphoreType.DMA(())],
        compiler_params=_params("parallel", "arbitrary"),
        name="ffn",
    )(x2d, gain, w_gate_up, w_gate_up, w_down, gain_final)


def _dft_matrices(n, scale):
    r = 1
    while r * r < n:
        r *= 2
    k = jnp.arange(n, dtype=jnp.int32)[:, None]
    hi = jnp.arange(n // r, dtype=jnp.int32)[None, :] * r
    lo = jnp.arange(r, dtype=jnp.int32)[None, :]
    w = 2.0 * math.pi / n
    a_hi = ((k * hi) % n).astype(F32) * w
    a_lo = ((k * lo) % n).astype(F32) * w
    ch, sh = jnp.cos(a_hi)[:, :, None], jnp.sin(a_hi)[:, :, None]
    cl, sl = jnp.cos(a_lo)[:, None, :], jnp.sin(a_lo)[:, None, :]
    c = (ch * cl - sh * sl).reshape(n, n) * scale
    s = (sh * cl + ch * sl).reshape(n, n) * scale
    return c.astype(BF16), s.astype(BF16)


def _dft_ch_kernel(x_ref, g_ref, cs_ref, hc_ref, hs_ref, h_ref, *, group_dim):
    _rmsnorm_block(x_ref, g_ref, h_ref, EPS)
    h = h_ref[...]
    cs = cs_ref[...]
    for g in range(h.shape[1] // group_dim):
        y = jnp.dot(h[:, g * group_dim:(g + 1) * group_dim], cs, preferred_element_type=F32)
        hc_ref[:, g * group_dim:(g + 1) * group_dim] = y[:, :group_dim].astype(hc_ref.dtype)
        hs_ref[:, g * group_dim:(g + 1) * group_dim] = y[:, group_dim:].astype(hs_ref.dtype)


def _dft_channels(x2d, gain, cs):
    m, d = x2d.shape
    gd = cs.shape[0]
    bm = _block(m, 512)
    row = pl.BlockSpec((bm, d), lambda i: (i, 0))
    return pl.pallas_call(
        functools.partial(_dft_ch_kernel, group_dim=gd),
        grid=(m // bm,),
        in_specs=[row, pl.BlockSpec((1, d), lambda i: (0, 0)), pl.BlockSpec(cs.shape, lambda i: (0, 0))],
        out_specs=[row, row],
        out_shape=[jax.ShapeDtypeStruct((m, d), BF16)] * 2,
        scratch_shapes=[pltpu.VMEM((bm, d), BF16)],
        compiler_params=_params("parallel"),
        name="dft_channels",
    )(x2d, gain, cs)


def _dft_pos_kernel(c_ref, sn_ref, hc_ref, hs_ref, o_ref):
    y = (jnp.dot(c_ref[...], hc_ref[...], preferred_element_type=F32)
         + jnp.dot(sn_ref[...], hs_ref[...], preferred_element_type=F32))
    o_ref[...] = y.astype(o_ref.dtype)


def _dft_positions(cmat, snmat, hc, hs, batch, seq):
    m, d = hc.shape
    bm = _block(seq, 512)
    bn = _block(d, 512)
    ni = seq // bm
    mat = pl.BlockSpec((bm, seq), lambda b, i, j: (i, 0))
    col = pl.BlockSpec((seq, bn), lambda b, i, j: (b, j))
    return pl.pallas_call(
        _dft_pos_kernel,
        grid=(batch, ni, d // bn),
        in_specs=[mat, mat, col, col],
        out_specs=pl.BlockSpec((bm, bn), lambda b, i, j: (b * ni + i, j)),
        out_shape=jax.ShapeDtypeStruct((m, d), BF16),
        compiler_params=_params("parallel", "parallel", "arbitrary"),
        name="dft_positions",
    )(cmat, snmat, hc, hs)


def _trunk(x, p):
    batch, seq, d = x.shape
    depth = p["norm_mix"].shape[0]
    x2d = x.reshape(batch * seq, d)
    rope = _rope_tables(seq)
    group_dim = d // N_FOURIER_GROUPS
    for i in range(depth):
        j = i // N_MIXERS
        gain_mix = p["norm_mix"][i][None, :]
        if i % N_MIXERS == 0:
            lambda_init = 0.8 - 0.6 * math.exp(-0.3 * i)
            qkv = _qkv_proj(x2d, gain_mix, p["w_qkv"][j], rope, seq)
            o = _diff_attention(qkv, p["lq1"][j][None, :], p["lk1"][j][None, :], p["lq2"][j][None, :],
                                p["lk2"][j][None, :], p["subln"][j][None, :], batch, seq, lambda_init)
            x2d = _proj_residual(o, p["w_o"][j], x2d)
        else:
            cc, sc = _dft_matrices(group_dim, group_dim ** -0.5)
            cp, sp = _dft_matrices(seq, seq ** -0.5)
            hc, hs = _dft_channels(x2d, gain_mix, jnp.concatenate([cc, sc], axis=1))
            f = _dft_positions(cp, -sp, hc, hs, batch, seq)
            x2d = _proj_residual(f, p["w_f"][j], x2d)
        last = i == depth - 1
        x2d = _ffn(x2d, p["norm_ffn"][i][None, :], p["w_gate_up"][i], p["w_down"][i],
                   p["norm_final"][None, :], final_norm=last)
    return x2d.reshape(batch, seq, d)


def kernel(x_prompt, x_sample, norm_mix, norm_ffn, norm_final, attn_w_qkv, attn_w_o, attn_lambda_q1,
           attn_lambda_k1, attn_lambda_q2, attn_lambda_k2, attn_subln, fourier_w_o, ffn_w_gate_up, ffn_w_down):
    assert x_prompt.shape[-1] % V_DIM == 0 and x_prompt.shape[-1] % (N_FOURIER_GROUPS * LANES) == 0
    p = {
        "norm_mix": norm_mix, "norm_ffn": norm_ffn, "norm_final": norm_final,
        "w_qkv": attn_w_qkv.astype(BF16), "w_o": attn_w_o.astype(BF16),
        "lq1": attn_lambda_q1, "lk1": attn_lambda_k1, "lq2": attn_lambda_q2, "lk2": attn_lambda_k2,
        "subln": attn_subln, "w_f": fourier_w_o.astype(BF16),
        "w_gate_up": ffn_w_gate_up.astype(BF16), "w_down": ffn_w_down.astype(BF16),
    }
    return (_trunk(x_prompt, p), _trunk(x_sample, p))
```

```python
import functools
import math

import numpy as np
import jax
import jax.numpy as jnp
from jax import lax
from jax.experimental import pallas as pl
from jax.experimental.pallas import tpu as pltpu

HEAD_DIM = 128
V_DIM = 2 * HEAD_DIM
ROT_DIM = HEAD_DIM // 4
ROT_HALF = ROT_DIM // 2
ROPE_THETA = 500000.0
N_FOURIER_GROUPS = 8
N_MIXERS = 2
EPS = 1e-6
SUBLN_EPS = 1e-5
LANES = 128
VMEM_LIMIT_BYTES = 58 * 1024 * 1024

BF16 = jnp.bfloat16
F32 = jnp.float32


def _block(n, want):
    b = min(n, want)
    while n % b:
        b //= 2
    return b


def _params(*sem):
    return pltpu.CompilerParams(dimension_semantics=sem, vmem_limit_bytes=VMEM_LIMIT_BYTES)


def _rmsnorm_rows(x, g, eps):
    return x * lax.rsqrt(jnp.mean(x * x, axis=-1, keepdims=True) + eps) * g


NORM_ROWS = 16
NORM_UNROLL = 4


def _rmsnorm_block(x_ref, g_ref, o_ref, eps):
    g = g_ref[...]

    def body(r, carry):
        rows = pl.ds(pl.multiple_of(r * NORM_ROWS, NORM_ROWS), NORM_ROWS)
        o_ref[rows, :] = _rmsnorm_rows(x_ref[rows, :], g, eps).astype(o_ref.dtype)
        return carry

    lax.fori_loop(0, x_ref.shape[0] // NORM_ROWS, body, 0, unroll=NORM_UNROLL)


ROPE_COLS = 512


def _qkv_kernel(x_ref, g_ref, w_ref, cos_ref, sa_ref, sb_ref, o_ref, h_ref, *, n_q_blocks, q_scale):
    j = pl.program_id(1)

    @pl.when(j == 0)
    def _():
        _rmsnorm_block(x_ref, g_ref, h_ref, EPS)

    def project(lo=0, width=None):
        width = w_ref.shape[1] if width is None else width
        return jnp.dot(h_ref[...], w_ref[:, lo:lo + width], preferred_element_type=F32)

    def rotary(scale):
        cos, sa, sb = cos_ref[...] * scale, sa_ref[...] * scale, sb_ref[...] * scale
        width = min(ROPE_COLS, w_ref.shape[1])
        for lo in range(0, w_ref.shape[1], width):
            acc = project(lo, width)
            for c in range(lo, lo + width, HEAD_DIM):
                t = acc[:, c - lo:c - lo + HEAD_DIM]
                r = t * cos + pltpu.roll(t, HEAD_DIM - ROT_HALF, 1) * sa + pltpu.roll(t, ROT_HALF, 1) * sb
                o_ref[:, c:c + HEAD_DIM] = r.astype(o_ref.dtype)

    @pl.when(j < n_q_blocks)
    def _():
        rotary(q_scale)

    @pl.when(jnp.logical_and(j >= n_q_blocks, j < 2 * n_q_blocks))
    def _():
        rotary(1.0)

    @pl.when(j >= 2 * n_q_blocks)
    def _():
        o_ref[...] = project().astype(o_ref.dtype)


def _qkv_proj(x2d, gain, w_qkv, rope, seq):
    m, d = x2d.shape
    n = w_qkv.shape[1]
    bm = _block(seq, 512)
    bn = _block(d, 1024)
    n_pos_blocks = seq // bm
    cos, sa, sb = rope
    rope_spec = pl.BlockSpec((bm, HEAD_DIM), lambda i, j: (i % n_pos_blocks, 0))
    return pl.pallas_call(
        functools.partial(_qkv_kernel, n_q_blocks=d // bn, q_scale=HEAD_DIM ** -0.5 * math.log2(math.e)),
        grid=(m // bm, n // bn),
        in_specs=[
            pl.BlockSpec((bm, d), lambda i, j: (i, 0)),
            pl.BlockSpec((1, d), lambda i, j: (0, 0)),
            pl.BlockSpec((d, bn), lambda i, j: (0, j)),
            rope_spec, rope_spec, rope_spec,
        ],
        out_specs=pl.BlockSpec((bm, bn), lambda i, j: (i, j)),
        out_shape=jax.ShapeDtypeStruct((m, n), BF16),
        scratch_shapes=[pltpu.VMEM((bm, d), BF16)],
        compiler_params=_params("parallel", "arbitrary"),
        name="qkv",
    )(x2d, gain, w_qkv, cos, sa, sb)


def _rope_tables(seq):
    inv_freq = ROPE_THETA ** (-jnp.arange(0, ROT_DIM, 2, dtype=F32) / ROT_DIM)
    ang = jnp.arange(seq, dtype=F32)[:, None] * inv_freq[None, :]
    c, s = jnp.cos(ang), jnp.sin(ang)
    pad = HEAD_DIM - ROT_DIM
    cos = jnp.concatenate([c, c, jnp.ones((seq, pad), F32)], axis=1)
    sa = jnp.concatenate([-s, jnp.zeros((seq, pad + ROT_HALF), F32)], axis=1)
    sb = jnp.concatenate([jnp.zeros((seq, ROT_HALF), F32), s, jnp.zeros((seq, pad), F32)], axis=1)
    return cos, sa, sb


ATTN_ROWS = 128


def _attn_kernel(q_ref, k_ref, v_ref, lq1_ref, lk1_ref, lq2_ref, lk2_ref, sub_ref, o_ref, *, lambda_init):
    lam = (jnp.exp(jnp.sum(lq1_ref[...] * lk1_ref[...], axis=-1, keepdims=True))
           - jnp.exp(jnp.sum(lq2_ref[...] * lk2_ref[...], axis=-1, keepdims=True))
           + lambda_init)
    k = k_ref[...]
    v = v_ref[...]
    nt = (((1,), (1,)), ((), ()))
    rows = min(ATTN_ROWS, q_ref.shape[0])
    tasks = [(r, c) for r in range(q_ref.shape[0] // rows) for c in range(2)]

    def scores(t):
        r, c = tasks[t]
        return lax.dot_general(q_ref[r * rows:(r + 1) * rows, c * HEAD_DIM:(c + 1) * HEAD_DIM],
                               k[:, c * HEAD_DIM:(c + 1) * HEAD_DIM], nt, preferred_element_type=F32)

    def exponentials(s):
        p = jnp.exp2(s - jnp.max(s, axis=-1, keepdims=True))
        return p.astype(BF16), jnp.sum(p, axis=-1, keepdims=True)

    n = len(tasks)
    s = {t: scores(t) for t in range(min(2, n))}
    p = {0: exponentials(s.pop(0))}
    pv, norm = {}, {}
    for t in range(n):
        pb, norm[t] = p.pop(t)
        pv[t] = jnp.dot(pb, v, preferred_element_type=F32)
        if t + 2 < n:
            s[t + 2] = scores(t + 2)
        if t + 1 < n:
            p[t + 1] = exponentials(s.pop(t + 1))
        if tasks[t][1] == 1:
            r = tasks[t][0]
            o = pv.pop(t - 1) * (1.0 / norm[t - 1]) - pv.pop(t) * (lam / norm[t])
            o = _rmsnorm_rows(o, sub_ref[...], SUBLN_EPS) * (1.0 - lambda_init)
            o_ref[r * rows:(r + 1) * rows, :] = o.astype(o_ref.dtype)


def _diff_attention(qkv, lq1, lk1, lq2, lk2, subln, batch, seq, lambda_init):
    m, n3 = qkv.shape
    d = n3 // 3
    heads = d // V_DIM
    bq = _block(seq, 1024)
    nq = seq // bq
    vec = pl.BlockSpec((1, HEAD_DIM), lambda b, h, i: (0, 0))
    return pl.pallas_call(
        functools.partial(_attn_kernel, lambda_init=lambda_init),
        grid=(batch, heads, nq),
        in_specs=[
            pl.BlockSpec((bq, V_DIM), lambda b, h, i: (b * nq + i, h)),
            pl.BlockSpec((seq, V_DIM), lambda b, h, i: (b, heads + h)),
            pl.BlockSpec((seq, V_DIM), lambda b, h, i: (b, 2 * heads + h)),
            vec, vec, vec, vec,
            pl.BlockSpec((1, V_DIM), lambda b, h, i: (0, 0)),
        ],
        out_specs=pl.BlockSpec((bq, V_DIM), lambda b, h, i: (b * nq + i, h)),
        out_shape=jax.ShapeDtypeStruct((m, d), BF16),
        compiler_params=_params("parallel", "parallel", "arbitrary"),
        name="diff_attn",
    )(qkv, qkv, qkv, lq1, lk1, lq2, lk2, subln)


def _proj_kernel(a_ref, w_ref, res_ref, o_ref):
    o_ref[...] = res_ref[...] + jnp.dot(a_ref[...], w_ref[...], preferred_element_type=F32)


def _proj_residual(a, w, res):
    m, k = a.shape
    n = w.shape[1]
    bm = _block(m, 1024)
    bn = _block(n, 512)
    return pl.pallas_call(
        _proj_kernel,
        grid=(m // bm, n // bn),
        in_specs=[
            pl.BlockSpec((bm, k), lambda i, j: (i, 0)),
            pl.BlockSpec((k, bn), lambda i, j: (0, j)),
            pl.BlockSpec((bm, bn), lambda i, j: (i, j)),
        ],
        out_specs=pl.BlockSpec((bm, bn), lambda i, j: (i, j)),
        out_shape=jax.ShapeDtypeStruct((m, n), F32),
        compiler_params=_params("parallel", "arbitrary"),
        name="proj_residual",
    )(a, w, res)


def _ffn_kernel(x_hbm, g_ref, wg_ref, wu_ref, wd_ref, gf_ref, o_ref, h_ref, sem, *, final_norm):
    i, j = pl.program_id(0), pl.program_id(1)
    bm = o_ref.shape[0]

    @pl.when(j == 0)
    def _():
        rows = pl.ds(pl.multiple_of(i * bm, bm), bm)
        fetch = pltpu.make_async_copy(x_hbm.at[rows, :], o_ref, sem)
        fetch.start()
        fetch.wait()
        _rmsnorm_block(o_ref, g_ref, h_ref, EPS)

    h = h_ref[...]
    gate = jnp.dot(h, wg_ref[...], preferred_element_type=F32)
    up = jnp.dot(h, wu_ref[...], preferred_element_type=F32)
    act = (gate / (1.0 + jnp.exp(-gate)) * up).astype(BF16)
    o_ref[...] += jnp.dot(act, wd_ref[...], preferred_element_type=F32)

    if final_norm:
        @pl.when(j == pl.num_programs(1) - 1)
        def _():
            _rmsnorm_block(o_ref, gf_ref, o_ref, EPS)


def _ffn(x2d, gain, w_gate_up, w_down, gain_final, final_norm):
    m, d = x2d.shape
    f = w_down.shape[0]
    bm = _block(m, 1024)
    bc = _block(f, 256)
    nf = f // bc
    return pl.pallas_call(
        functools.partial(_ffn_kernel, final_norm=final_norm),
        grid=(m // bm, nf),
        in_specs=[
            pl.BlockSpec(memory_space=pl.ANY),
            pl.BlockSpec((1, d), lambda i, j: (0, 0)),
            pl.BlockSpec((d, bc), lambda i, j: (0, j)),
            pl.BlockSpec((d, bc), lambda i, j: (0, nf + j)),
            pl.BlockSpec((bc, d), lambda i, j: (j, 0)),
            pl.BlockSpec((1, d), lambda i, j: (0, 0)),
        ],
        out_specs=pl.BlockSpec((bm, d), lambda i, j: (i, 0)),
        out_shape=jax.ShapeDtypeStruct((m, d), F32),
        scratch_shapes=[pltpu.VMEM((bm, d), BF16), pltpu.SemaphoreType.DMA(())],
        compiler_params=_params("parallel", "arbitrary"),
        name="ffn",
    )(x2d, gain, w_gate_up, w_gate_up, w_down, gain_final)


def _dft_matrices(n, scale):
    r = 1
    while r * r < n:
        r *= 2
    k = jnp.arange(n, dtype=jnp.int32)[:, None]
    hi = jnp.arange(n // r, dtype=jnp.int32)[None, :] * r
    lo = jnp.arange(r, dtype=jnp.int32)[None, :]
    w = 2.0 * math.pi / n
    a_hi = ((k * hi) % n).astype(F32) * w
    a_lo = ((k * lo) % n).astype(F32) * w
    ch, sh = jnp.cos(a_hi)[:, :, None], jnp.sin(a_hi)[:, :, None]
    cl, sl = jnp.cos(a_lo)[:, None, :], jnp.sin(a_lo)[:, None, :]
    c = (ch * cl - sh * sl).reshape(n, n) * scale
    s = (sh * cl + ch * sl).reshape(n, n) * scale
    return c.astype(BF16), s.astype(BF16)


RADIX = 4
DFT_ROWS = 128


def _twiddle_table(seq):
    n = jnp.arange(seq // RADIX, dtype=jnp.int32)[:, None]
    rho = jnp.arange(1, RADIX, dtype=jnp.int32)[None, :]
    ang = (n * rho).astype(F32) * (2.0 * math.pi / seq)
    cs = jnp.stack([jnp.cos(ang), jnp.sin(ang)], axis=-1).reshape(seq // RADIX, 2 * (RADIX - 1))
    return jnp.concatenate([cs, jnp.zeros((seq // RADIX, 2), F32)], axis=1)


def _dft_ch_kernel(x0_ref, x1_ref, x2_ref, x3_ref, g_ref, cs_ref, tw_ref, zr_ref, zi_ref,
                   sum_ref, alt_ref, d02_ref, d13_ref, *, group_dim):
    g = g_ref[...]

    def body(r, carry):
        rows = pl.ds(pl.multiple_of(r * NORM_ROWS, NORM_ROWS), NORM_ROWS)
        h0 = _rmsnorm_rows(x0_ref[rows, :], g, EPS)
        h1 = _rmsnorm_rows(x1_ref[rows, :], g, EPS)
        h2 = _rmsnorm_rows(x2_ref[rows, :], g, EPS)
        h3 = _rmsnorm_rows(x3_ref[rows, :], g, EPS)
        even, odd = h0 + h2, h1 + h3
        sum_ref[rows, :] = (even + odd).astype(BF16)
        alt_ref[rows, :] = (even - odd).astype(BF16)
        d02_ref[rows, :] = (h0 - h2).astype(BF16)
        d13_ref[rows, :] = (h1 - h3).astype(BF16)
        return carry

    lax.fori_loop(0, x0_ref.shape[0] // NORM_ROWS, body, 0)

    cs = cs_ref[...]
    tw = tw_ref[...]
    gd = group_dim
    for grp in range(sum_ref.shape[1] // gd):
        cols = slice(grp * gd, (grp + 1) * gd)
        p = jnp.dot(sum_ref[:, cols], cs, preferred_element_type=F32)
        m = jnp.dot(alt_ref[:, cols], cs, preferred_element_type=F32)
        c = jnp.dot(d02_ref[:, cols], cs, preferred_element_type=F32)
        d = jnp.dot(d13_ref[:, cols], cs, preferred_element_type=F32)
        u = [(p[:, :gd], -p[:, gd:]),
             (c[:, :gd] - d[:, gd:], -(c[:, gd:] + d[:, :gd])),
             (m[:, :gd], -m[:, gd:]),
             (c[:, :gd] + d[:, gd:], d[:, :gd] - c[:, gd:])]
        for rho, (ur, ui) in enumerate(u):
            if rho:
                ct, st = tw[:, 2 * rho - 2:2 * rho - 1], tw[:, 2 * rho - 1:2 * rho]
                ur, ui = ur * ct + ui * st, ui * ct - ur * st
            zr_ref[rho, :, cols] = ur.astype(zr_ref.dtype)
            zi_ref[rho, :, cols] = ui.astype(zi_ref.dtype)


def _dft_channels(x2d, gain, cs, tw, batch, seq):
    m, d = x2d.shape
    gd = cs.shape[0]
    quarter = seq // RADIX
    bm = _block(quarter, DFT_ROWS)
    ni = quarter // bm

    def x_spec(q):
        return pl.BlockSpec((bm, d), lambda b, i: (b * RADIX * ni + q * ni + i, 0))

    z_spec = pl.BlockSpec((RADIX, bm, d), lambda b, i: (0, b * ni + i, 0))
    return pl.pallas_call(
        functools.partial(_dft_ch_kernel, group_dim=gd),
        grid=(batch, ni),
        in_specs=[x_spec(0), x_spec(1), x_spec(2), x_spec(3),
                  pl.BlockSpec((1, d), lambda b, i: (0, 0)),
                  pl.BlockSpec(cs.shape, lambda b, i: (0, 0)),
                  pl.BlockSpec((bm, tw.shape[1]), lambda b, i: (i, 0))],
        out_specs=[z_spec, z_spec],
        out_shape=[jax.ShapeDtypeStruct((RADIX, m // RADIX, d), BF16)] * 2,
        scratch_shapes=[pltpu.VMEM((bm, d), BF16)] * 4,
        compiler_params=_params("parallel", "arbitrary"),
        name="dft_channels",
    )(x2d, x2d, x2d, x2d, gain, cs, tw)


def _dft_pos_kernel(c_ref, s_ref, zr_ref, zi_ref, o_ref):
    y = (jnp.dot(c_ref[...], zr_ref[...], preferred_element_type=F32)
         + jnp.dot(s_ref[...], zi_ref[...], preferred_element_type=F32))
    o_ref[...] = y.astype(o_ref.dtype)


def _dft_positions(cmat, smat, zr, zi, batch):
    _, mq, d = zr.shape
    quarter = cmat.shape[0]
    bn = _block(d, max(LANES, (2 * 1024 * 1024) // quarter))
    nj = d // bn
    mat = pl.BlockSpec((quarter, quarter), lambda r, b, j: (0, 0))
    col = pl.BlockSpec((None, quarter, bn), lambda r, b, j: (r, b, j))
    out = pl.pallas_call(
        _dft_pos_kernel,
        grid=(RADIX, batch, nj),
        in_specs=[mat, mat, col, col],
        out_specs=pl.BlockSpec((quarter, bn), lambda r, b, j: (b, r * nj + j)),
        out_shape=jax.ShapeDtypeStruct((mq, RADIX * d), BF16),
        compiler_params=_params("parallel", "parallel", "arbitrary"),
        name="dft_positions",
    )(cmat, smat, zr, zi)
    return out.reshape(mq * RADIX, d)


def _trunk(x, p):
    batch, seq, d = x.shape
    depth = p["norm_mix"].shape[0]
    x2d = x.reshape(batch * seq, d)
    rope = _rope_tables(seq)
    group_dim = d // N_FOURIER_GROUPS
    for i in range(depth):
        j = i // N_MIXERS
        gain_mix = p["norm_mix"][i][None, :]
        if i % N_MIXERS == 0:
            lambda_init = 0.8 - 0.6 * math.exp(-0.3 * i)
            qkv = _qkv_proj(x2d, gain_mix, p["w_qkv"][j], rope, seq)
            o = _diff_attention(qkv, p["lq1"][j][None, :], p["lk1"][j][None, :], p["lq2"][j][None, :],
                                p["lk2"][j][None, :], p["subln"][j][None, :], batch, seq, lambda_init)
            x2d = _proj_residual(o, p["w_o"][j], x2d)
        else:
            cc, sc = _dft_matrices(group_dim, group_dim ** -0.5)
            cq, sq = _dft_matrices(seq // RADIX, seq ** -0.5)
            zr, zi = _dft_channels(x2d, gain_mix, jnp.concatenate([cc, sc], axis=1), _twiddle_table(seq),
                                   batch, seq)
            f = _dft_positions(cq, sq, zr, zi, batch)
            x2d = _proj_residual(f, p["w_f"][j], x2d)
        last = i == depth - 1
        x2d = _ffn(x2d, p["norm_ffn"][i][None, :], p["w_gate_up"][i], p["w_down"][i],
                   p["norm_final"][None, :], final_norm=last)
    return x2d.reshape(batch, seq, d)


def kernel(x_prompt, x_sample, norm_mix, norm_ffn, norm_final, attn_w_qkv, attn_w_o, attn_lambda_q1,
           attn_lambda_k1, attn_lambda_q2, attn_lambda_k2, attn_subln, fourier_w_o, ffn_w_gate_up, ffn_w_down):
    assert x_prompt.shape[-1] % V_DIM == 0 and x_prompt.shape[-1] % (N_FOURIER_GROUPS * LANES) == 0
    def per_layer_bf16(w):
        return [w[layer].astype(BF16) for layer in range(w.shape[0])]

    p = {
        "norm_mix": norm_mix, "norm_ffn": norm_ffn, "norm_final": norm_final,
        "w_qkv": per_layer_bf16(attn_w_qkv), "w_o": per_layer_bf16(attn_w_o),
        "lq1": attn_lambda_q1, "lk1": attn_lambda_k1, "lq2": attn_lambda_q2, "lk2": attn_lambda_k2,
        "subln": attn_subln, "w_f": per_layer_bf16(fourier_w_o),
        "w_gate_up": per_layer_bf16(ffn_w_gate_up), "w_down": per_layer_bf16(ffn_w_down),
    }
    return (_trunk(x_prompt, p), _trunk(x_sample, p))
```

```python
import functools
import math

import numpy as np
import jax
import jax.numpy as jnp
from jax import lax
from jax.experimental import pallas as pl
from jax.experimental.pallas import tpu as pltpu

HEAD_DIM = 128
V_DIM = 2 * HEAD_DIM
ROT_DIM = HEAD_DIM // 4
ROT_HALF = ROT_DIM // 2
ROPE_THETA = 500000.0
N_FOURIER_GROUPS = 8
N_MIXERS = 2
EPS = 1e-6
SUBLN_EPS = 1e-5
LANES = 128
VMEM_LIMIT_BYTES = 58 * 1024 * 1024

BF16 = jnp.bfloat16
F32 = jnp.float32


def _block(n, want):
    b = min(n, want)
    while n % b:
        b //= 2
    return b


def _params(*sem):
    return pltpu.CompilerParams(dimension_semantics=sem, vmem_limit_bytes=VMEM_LIMIT_BYTES)


def _rmsnorm_rows(x, g, eps):
    return x * lax.rsqrt(jnp.mean(x * x, axis=-1, keepdims=True) + eps) * g


NORM_ROWS = 16
NORM_UNROLL = 4


def _rmsnorm_block(x_ref, g_ref, o_ref, eps):
    g = g_ref[...]
    step = NORM_ROWS * NORM_UNROLL

    def body(r, carry):
        base = r * step
        groups = [pl.ds(pl.multiple_of(base + u * NORM_ROWS, NORM_ROWS), NORM_ROWS) for u in range(NORM_UNROLL)]
        xs = [x_ref[rows, :] for rows in groups]
        for rows, x in zip(groups, xs):
            o_ref[rows, :] = _rmsnorm_rows(x, g, eps).astype(o_ref.dtype)
        return carry

    lax.fori_loop(0, x_ref.shape[0] // step, body, 0)


ROPE_COLS = 512


def _qkv_kernel(x_ref, g_ref, w_ref, cos_ref, sa_ref, sb_ref, o_ref, h_ref, *, n_q_blocks, q_scale):
    j = pl.program_id(1)

    @pl.when(j == 0)
    def _():
        _rmsnorm_block(x_ref, g_ref, h_ref, EPS)

    def project(lo=0, width=None):
        width = w_ref.shape[1] if width is None else width
        return jnp.dot(h_ref[...], w_ref[:, lo:lo + width], preferred_element_type=F32)

    def rotary(scale):
        cos, sa, sb = cos_ref[...] * scale, sa_ref[...] * scale, sb_ref[...] * scale
        width = min(ROPE_COLS, w_ref.shape[1])
        for lo in range(0, w_ref.shape[1], width):
            acc = project(lo, width)
            for c in range(lo, lo + width, HEAD_DIM):
                t = acc[:, c - lo:c - lo + HEAD_DIM]
                r = t * cos + pltpu.roll(t, HEAD_DIM - ROT_HALF, 1) * sa + pltpu.roll(t, ROT_HALF, 1) * sb
                o_ref[:, c:c + HEAD_DIM] = r.astype(o_ref.dtype)

    @pl.when(j < n_q_blocks)
    def _():
        rotary(q_scale)

    @pl.when(jnp.logical_and(j >= n_q_blocks, j < 2 * n_q_blocks))
    def _():
        rotary(1.0)

    @pl.when(j >= 2 * n_q_blocks)
    def _():
        o_ref[...] = project().astype(o_ref.dtype)


def _qkv_proj(x2d, gain, w_qkv, rope, seq):
    m, d = x2d.shape
    n = w_qkv.shape[1]
    bm = _block(seq, 512)
    bn = _block(d, 1024)
    n_pos_blocks = seq // bm
    cos, sa, sb = rope
    rope_spec = pl.BlockSpec((bm, HEAD_DIM), lambda i, j: (i % n_pos_blocks, 0))
    return pl.pallas_call(
        functools.partial(_qkv_kernel, n_q_blocks=d // bn, q_scale=HEAD_DIM ** -0.5 * math.log2(math.e)),
        grid=(m // bm, n // bn),
        in_specs=[
            pl.BlockSpec((bm, d), lambda i, j: (i, 0)),
            pl.BlockSpec((1, d), lambda i, j: (0, 0)),
            pl.BlockSpec((d, bn), lambda i, j: (0, j)),
            rope_spec, rope_spec, rope_spec,
        ],
        out_specs=pl.BlockSpec((bm, bn), lambda i, j: (i, j)),
        out_shape=jax.ShapeDtypeStruct((m, n), BF16),
        scratch_shapes=[pltpu.VMEM((bm, d), BF16)],
        compiler_params=_params("parallel", "arbitrary"),
        name="qkv",
    )(x2d, gain, w_qkv, cos, sa, sb)


def _rope_tables(seq):
    inv_freq = ROPE_THETA ** (-jnp.arange(0, ROT_DIM, 2, dtype=F32) / ROT_DIM)
    ang = jnp.arange(seq, dtype=F32)[:, None] * inv_freq[None, :]
    c, s = jnp.cos(ang), jnp.sin(ang)
    pad = HEAD_DIM - ROT_DIM
    cos = jnp.concatenate([c, c, jnp.ones((seq, pad), F32)], axis=1)
    sa = jnp.concatenate([-s, jnp.zeros((seq, pad + ROT_HALF), F32)], axis=1)
    sb = jnp.concatenate([jnp.zeros((seq, ROT_HALF), F32), s, jnp.zeros((seq, pad), F32)], axis=1)
    return cos, sa, sb


ATTN_ROWS = 128


def _attn_kernel(q_ref, k_ref, v_ref, lq1_ref, lk1_ref, lq2_ref, lk2_ref, sub_ref, o_ref, *, lambda_init):
    lam = (jnp.exp(jnp.sum(lq1_ref[...] * lk1_ref[...], axis=-1, keepdims=True))
           - jnp.exp(jnp.sum(lq2_ref[...] * lk2_ref[...], axis=-1, keepdims=True))
           + lambda_init)
    k = k_ref[...]
    v = v_ref[...]
    nt = (((1,), (1,)), ((), ()))
    rows = min(ATTN_ROWS, q_ref.shape[0])
    tasks = [(r, c) for r in range(q_ref.shape[0] // rows) for c in range(2)]

    def scores(t):
        r, c = tasks[t]
        return lax.dot_general(q_ref[r * rows:(r + 1) * rows, c * HEAD_DIM:(c + 1) * HEAD_DIM],
                               k[:, c * HEAD_DIM:(c + 1) * HEAD_DIM], nt, preferred_element_type=F32)

    def exponentials(s):
        p = jnp.exp2(s - jnp.max(s, axis=-1, keepdims=True))
        return p.astype(BF16), jnp.sum(p, axis=-1, keepdims=True)

    n = len(tasks)
    s = {t: scores(t) for t in range(min(2, n))}
    p = {0: exponentials(s.pop(0))}
    pv, norm = {}, {}
    for t in range(n):
        pb, norm[t] = p.pop(t)
        pv[t] = jnp.dot(pb, v, preferred_element_type=F32)
        if t + 2 < n:
            s[t + 2] = scores(t + 2)
        if t + 1 < n:
            p[t + 1] = exponentials(s.pop(t + 1))
        if tasks[t][1] == 1:
            r = tasks[t][0]
            o = pv.pop(t - 1) * (1.0 / norm[t - 1]) - pv.pop(t) * (lam / norm[t])
            o = _rmsnorm_rows(o, sub_ref[...], SUBLN_EPS) * (1.0 - lambda_init)
            o_ref[r * rows:(r + 1) * rows, :] = o.astype(o_ref.dtype)


def _diff_attention(qkv, lq1, lk1, lq2, lk2, subln, batch, seq, lambda_init):
    m, n3 = qkv.shape
    d = n3 // 3
    heads = d // V_DIM
    bq = _block(seq, 1024)
    nq = seq // bq
    vec = pl.BlockSpec((1, HEAD_DIM), lambda b, h, i: (0, 0))
    return pl.pallas_call(
        functools.partial(_attn_kernel, lambda_init=lambda_init),
        grid=(batch, heads, nq),
        in_specs=[
            pl.BlockSpec((bq, V_DIM), lambda b, h, i: (b * nq + i, h)),
            pl.BlockSpec((seq, V_DIM), lambda b, h, i: (b, heads + h)),
            pl.BlockSpec((seq, V_DIM), lambda b, h, i: (b, 2 * heads + h)),
            vec, vec, vec, vec,
            pl.BlockSpec((1, V_DIM), lambda b, h, i: (0, 0)),
        ],
        out_specs=pl.BlockSpec((bq, V_DIM), lambda b, h, i: (b * nq + i, h)),
        out_shape=jax.ShapeDtypeStruct((m, d), BF16),
        compiler_params=_params("parallel", "parallel", "arbitrary"),
        name="diff_attn",
    )(qkv, qkv, qkv, lq1, lk1, lq2, lk2, subln)


def _proj_kernel(a_ref, w_ref, res_ref, o_ref):
    o_ref[...] = res_ref[...] + jnp.dot(a_ref[...], w_ref[...], preferred_element_type=F32)


def _proj_residual(a, w, res):
    m, k = a.shape
    n = w.shape[1]
    bm = _block(m, 1024)
    bn = _block(n, 512)
    return pl.pallas_call(
        _proj_kernel,
        grid=(m // bm, n // bn),
        in_specs=[
            pl.BlockSpec((bm, k), lambda i, j: (i, 0)),
            pl.BlockSpec((k, bn), lambda i, j: (0, j)),
            pl.BlockSpec((bm, bn), lambda i, j: (i, j)),
        ],
        out_specs=pl.BlockSpec((bm, bn), lambda i, j: (i, j)),
        out_shape=jax.ShapeDtypeStruct((m, n), F32),
        compiler_params=_params("parallel", "arbitrary"),
        name="proj_residual",
    )(a, w, res)


def _ffn_kernel(x_hbm, g_ref, wg_ref, wu_ref, wd_ref, gf_ref, o_ref, h_ref, sem, *, final_norm):
    i, j = pl.program_id(0), pl.program_id(1)
    bm = o_ref.shape[0]

    @pl.when(j == 0)
    def _():
        rows = pl.ds(pl.multiple_of(i * bm, bm), bm)
        fetch = pltpu.make_async_copy(x_hbm.at[rows, :], o_ref, sem)
        fetch.start()
        fetch.wait()
        _rmsnorm_block(o_ref, g_ref, h_ref, EPS)

    h = h_ref[...]
    gate = jnp.dot(h, wg_ref[...], preferred_element_type=F32)
    up = jnp.dot(h, wu_ref[...], preferred_element_type=F32)
    act = (gate / (1.0 + jnp.exp(-gate)) * up).astype(BF16)
    o_ref[...] += jnp.dot(act, wd_ref[...], preferred_element_type=F32)

    if final_norm:
        @pl.when(j == pl.num_programs(1) - 1)
        def _():
            _rmsnorm_block(o_ref, gf_ref, o_ref, EPS)


def _ffn(x2d, gain, w_gate_up, w_down, layer, gain_final, final_norm):
    m, d = x2d.shape
    f = w_down.shape[1]
    bm = _block(m, 1024)
    bc = _block(f, 256)
    nf = f // bc
    return pl.pallas_call(
        functools.partial(_ffn_kernel, final_norm=final_norm),
        grid=(m // bm, nf),
        in_specs=[
            pl.BlockSpec(memory_space=pl.ANY),
            pl.BlockSpec((1, d), lambda i, j: (0, 0)),
            pl.BlockSpec((None, d, bc), lambda i, j: (layer, 0, j)),
            pl.BlockSpec((None, d, bc), lambda i, j: (layer, 0, nf + j)),
            pl.BlockSpec((None, bc, d), lambda i, j: (layer, j, 0)),
            pl.BlockSpec((1, d), lambda i, j: (0, 0)),
        ],
        out_specs=pl.BlockSpec((bm, d), lambda i, j: (i, 0)),
        out_shape=jax.ShapeDtypeStruct((m, d), F32),
        scratch_shapes=[pltpu.VMEM((bm, d), BF16), pltpu.SemaphoreType.DMA(())],
        compiler_params=_params("parallel", "arbitrary"),
        name="ffn",
    )(x2d, gain, w_gate_up, w_gate_up, w_down, gain_final)


def _dft_matrices(n, scale):
    r = 1
    while r * r < n:
        r *= 2
    k = jnp.arange(n, dtype=jnp.int32)[:, None]
    hi = jnp.arange(n // r, dtype=jnp.int32)[None, :] * r
    lo = jnp.arange(r, dtype=jnp.int32)[None, :]
    w = 2.0 * math.pi / n
    a_hi = ((k * hi) % n).astype(F32) * w
    a_lo = ((k * lo) % n).astype(F32) * w
    ch, sh = jnp.cos(a_hi)[:, :, None], jnp.sin(a_hi)[:, :, None]
    cl, sl = jnp.cos(a_lo)[:, None, :], jnp.sin(a_lo)[:, None, :]
    c = (ch * cl - sh * sl).reshape(n, n) * scale
    s = (sh * cl + ch * sl).reshape(n, n) * scale
    return c.astype(BF16), s.astype(BF16)


RADIX = 4
DFT_ROWS = 128


def _twiddle_table(seq):
    n = jnp.arange(seq // RADIX, dtype=jnp.int32)[:, None]
    rho = jnp.arange(1, RADIX, dtype=jnp.int32)[None, :]
    ang = (n * rho).astype(F32) * (2.0 * math.pi / seq)
    cs = jnp.stack([jnp.cos(ang), jnp.sin(ang)], axis=-1).reshape(seq // RADIX, 2 * (RADIX - 1))
    return jnp.concatenate([cs, jnp.zeros((seq // RADIX, 2), F32)], axis=1)


def _dft_ch_kernel(x0_ref, x1_ref, x2_ref, x3_ref, g_ref, cs_ref, tw_ref, zr_ref, zi_ref,
                   sum_ref, alt_ref, d02_ref, d13_ref, *, group_dim):
    g = g_ref[...]

    def body(r, carry):
        rows = pl.ds(pl.multiple_of(r * NORM_ROWS, NORM_ROWS), NORM_ROWS)
        h0 = _rmsnorm_rows(x0_ref[rows, :], g, EPS)
        h1 = _rmsnorm_rows(x1_ref[rows, :], g, EPS)
        h2 = _rmsnorm_rows(x2_ref[rows, :], g, EPS)
        h3 = _rmsnorm_rows(x3_ref[rows, :], g, EPS)
        even, odd = h0 + h2, h1 + h3
        sum_ref[rows, :] = (even + odd).astype(BF16)
        alt_ref[rows, :] = (even - odd).astype(BF16)
        d02_ref[rows, :] = (h0 - h2).astype(BF16)
        d13_ref[rows, :] = (h1 - h3).astype(BF16)
        return carry

    lax.fori_loop(0, x0_ref.shape[0] // NORM_ROWS, body, 0)

    cs = cs_ref[...]
    tw = tw_ref[...]
    gd = group_dim
    for grp in range(sum_ref.shape[1] // gd):
        cols = slice(grp * gd, (grp + 1) * gd)
        p = jnp.dot(sum_ref[:, cols], cs, preferred_element_type=F32)
        m = jnp.dot(alt_ref[:, cols], cs, preferred_element_type=F32)
        c = jnp.dot(d02_ref[:, cols], cs, preferred_element_type=F32)
        d = jnp.dot(d13_ref[:, cols], cs, preferred_element_type=F32)
        u = [(p[:, :gd], -p[:, gd:]),
             (c[:, :gd] - d[:, gd:], -(c[:, gd:] + d[:, :gd])),
             (m[:, :gd], -m[:, gd:]),
             (c[:, :gd] + d[:, gd:], d[:, :gd] - c[:, gd:])]
        for rho, (ur, ui) in enumerate(u):
            if rho:
                ct, st = tw[:, 2 * rho - 2:2 * rho - 1], tw[:, 2 * rho - 1:2 * rho]
                ur, ui = ur * ct + ui * st, ui * ct - ur * st
            zr_ref[rho, :, cols] = ur.astype(zr_ref.dtype)
            zi_ref[rho, :, cols] = ui.astype(zi_ref.dtype)


def _dft_channels(x2d, gain, cs, tw, batch, seq):
    m, d = x2d.shape
    gd = cs.shape[0]
    quarter = seq // RADIX
    bm = _block(quarter, DFT_ROWS)
    ni = quarter // bm

    def x_spec(q):
        return pl.BlockSpec((bm, d), lambda b, i: (b * RADIX * ni + q * ni + i, 0))

    z_spec = pl.BlockSpec((RADIX, bm, d), lambda b, i: (0, b * ni + i, 0))
    return pl.pallas_call(
        functools.partial(_dft_ch_kernel, group_dim=gd),
        grid=(batch, ni),
        in_specs=[x_spec(0), x_spec(1), x_spec(2), x_spec(3),
                  pl.BlockSpec((1, d), lambda b, i: (0, 0)),
                  pl.BlockSpec(cs.shape, lambda b, i: (0, 0)),
                  pl.BlockSpec((bm, tw.shape[1]), lambda b, i: (i, 0))],
        out_specs=[z_spec, z_spec],
        out_shape=[jax.ShapeDtypeStruct((RADIX, m // RADIX, d), BF16)] * 2,
        scratch_shapes=[pltpu.VMEM((bm, d), BF16)] * 4,
        compiler_params=_params("parallel", "arbitrary"),
        name="dft_channels",
    )(x2d, x2d, x2d, x2d, gain, cs, tw)


def _dft_pos_kernel(c_ref, s_ref, zr_ref, zi_ref, o_ref, rows_ref):
    cmat, smat = c_ref[...], s_ref[...]
    quarter = cmat.shape[0]
    chunks = o_ref.shape[1] // LANES
    for rho in range(RADIX):
        y = (jnp.dot(cmat, zr_ref[rho], preferred_element_type=F32)
             + jnp.dot(smat, zi_ref[rho], preferred_element_type=F32))
        for c in range(chunks):
            rows_ref[c, pl.ds(rho, quarter, stride=RADIX), :] = y[:, c * LANES:(c + 1) * LANES]
    for c in range(chunks):
        o_ref[:, c * LANES:(c + 1) * LANES] = rows_ref[c].astype(o_ref.dtype)


def _dft_positions(cmat, smat, zr, zi, batch):
    _, mq, d = zr.shape
    quarter = cmat.shape[0]
    bn = _block(d, max(LANES, (512 * 1024) // quarter))
    mat = pl.BlockSpec((quarter, quarter), lambda b, j: (0, 0))
    col = pl.BlockSpec((RADIX, quarter, bn), lambda b, j: (0, b, j))
    return pl.pallas_call(
        _dft_pos_kernel,
        grid=(batch, d // bn),
        in_specs=[mat, mat, col, col],
        out_specs=pl.BlockSpec((RADIX * quarter, bn), lambda b, j: (b, j)),
        out_shape=jax.ShapeDtypeStruct((mq * RADIX, d), BF16),
        scratch_shapes=[pltpu.VMEM((bn // LANES, RADIX * quarter, LANES), F32)],
        compiler_params=_params("parallel", "arbitrary"),
        name="dft_positions",
    )(cmat, smat, zr, zi)


def _trunk(x, p):
    batch, seq, d = x.shape
    depth = p["norm_mix"].shape[0]
    x2d = x.reshape(batch * seq, d)
    rope = _rope_tables(seq)
    group_dim = d // N_FOURIER_GROUPS
    for i in range(depth):
        j = i // N_MIXERS
        gain_mix = p["norm_mix"][i][None, :]
        if i % N_MIXERS == 0:
            lambda_init = 0.8 - 0.6 * math.exp(-0.3 * i)
            qkv = _qkv_proj(x2d, gain_mix, p["w_qkv"][j], rope, seq)
            o = _diff_attention(qkv, p["lq1"][j][None, :], p["lk1"][j][None, :], p["lq2"][j][None, :],
                                p["lk2"][j][None, :], p["subln"][j][None, :], batch, seq, lambda_init)
            x2d = _proj_residual(o, p["w_o"][j], x2d)
        else:
            cc, sc = _dft_matrices(group_dim, group_dim ** -0.5)
            cq, sq = _dft_matrices(seq // RADIX, seq ** -0.5)
            zr, zi = _dft_channels(x2d, gain_mix, jnp.concatenate([cc, sc], axis=1), _twiddle_table(seq),
                                   batch, seq)
            f = _dft_positions(cq, sq, zr, zi, batch)
            x2d = _proj_residual(f, p["w_f"][j], x2d)
        last = i == depth - 1
        x2d = _ffn(x2d, p["norm_ffn"][i][None, :], p["w_gate_up"], p["w_down"], i,
                   p["norm_final"][None, :], final_norm=last)
    return x2d.reshape(batch, seq, d)


def kernel(x_prompt, x_sample, norm_mix, norm_ffn, norm_final, attn_w_qkv, attn_w_o, attn_lambda_q1,
           attn_lambda_k1, attn_lambda_q2, attn_lambda_k2, attn_subln, fourier_w_o, ffn_w_gate_up, ffn_w_down):
    assert x_prompt.shape[-1] % V_DIM == 0 and x_prompt.shape[-1] % (N_FOURIER_GROUPS * LANES) == 0
    p = {
        "norm_mix": norm_mix, "norm_ffn": norm_ffn, "norm_final": norm_final,
        "w_qkv": attn_w_qkv.astype(BF16), "w_o": attn_w_o.astype(BF16),
        "lq1": attn_lambda_q1, "lk1": attn_lambda_k1, "lq2": attn_lambda_q2, "lk2": attn_lambda_k2,
        "subln": attn_subln, "w_f": fourier_w_o.astype(BF16),
        "w_gate_up": ffn_w_gate_up.astype(BF16), "w_down": ffn_w_down.astype(BF16),
    }
    return (_trunk(x_prompt, p), _trunk(x_sample, p))
```

```python
import functools
import math

import numpy as np
import jax
import jax.numpy as jnp
from jax import lax
from jax.experimental import pallas as pl
from jax.experimental.pallas import tpu as pltpu

HEAD_DIM = 128
V_DIM = 2 * HEAD_DIM
ROT_DIM = HEAD_DIM // 4
ROT_HALF = ROT_DIM // 2
ROPE_THETA = 500000.0
N_FOURIER_GROUPS = 8
N_MIXERS = 2
EPS = 1e-6
SUBLN_EPS = 1e-5
LANES = 128
VMEM_LIMIT_BYTES = 58 * 1024 * 1024

BF16 = jnp.bfloat16
F32 = jnp.float32


def _block(n, want):
    b = min(n, want)
    while n % b:
        b //= 2
    return b


def _params(*sem):
    return pltpu.CompilerParams(dimension_semantics=sem, vmem_limit_bytes=VMEM_LIMIT_BYTES)


def _rmsnorm_rows(x, g, eps):
    return x * lax.rsqrt(jnp.mean(x * x, axis=-1, keepdims=True) + eps) * g


NORM_ROWS = 16
NORM_UNROLL = 4


def _rmsnorm_block(x_ref, g_ref, o_ref, eps):
    g = g_ref[...]
    step = NORM_ROWS * NORM_UNROLL

    def body(r, carry):
        base = r * step
        groups = [pl.ds(pl.multiple_of(base + u * NORM_ROWS, NORM_ROWS), NORM_ROWS) for u in range(NORM_UNROLL)]
        xs = [x_ref[rows, :] for rows in groups]
        for rows, x in zip(groups, xs):
            o_ref[rows, :] = _rmsnorm_rows(x, g, eps).astype(o_ref.dtype)
        return carry

    lax.fori_loop(0, x_ref.shape[0] // step, body, 0)


ROPE_COLS = 512


def _qkv_kernel(x_ref, g_ref, w_ref, cos_ref, sa_ref, sb_ref, o_ref, h_ref, *, n_q_blocks, q_scale):
    j = pl.program_id(1)

    @pl.when(j == 0)
    def _():
        _rmsnorm_block(x_ref, g_ref, h_ref, EPS)

    def project(lo=0, width=None):
        width = w_ref.shape[1] if width is None else width
        return jnp.dot(h_ref[...], w_ref[:, lo:lo + width], preferred_element_type=F32)

    def rotary(scale):
        cos, sa, sb = cos_ref[...] * scale, sa_ref[...] * scale, sb_ref[...] * scale
        width = min(ROPE_COLS, w_ref.shape[1])
        for lo in range(0, w_ref.shape[1], width):
            acc = project(lo, width)
            for c in range(lo, lo + width, HEAD_DIM):
                t = acc[:, c - lo:c - lo + HEAD_DIM]
                r = t * cos + pltpu.roll(t, HEAD_DIM - ROT_HALF, 1) * sa + pltpu.roll(t, ROT_HALF, 1) * sb
                o_ref[:, c:c + HEAD_DIM] = r.astype(o_ref.dtype)

    @pl.when(j < n_q_blocks)
    def _():
        rotary(q_scale)

    @pl.when(jnp.logical_and(j >= n_q_blocks, j < 2 * n_q_blocks))
    def _():
        rotary(1.0)

    @pl.when(j >= 2 * n_q_blocks)
    def _():
        o_ref[...] = project().astype(o_ref.dtype)


def _qkv_proj(x2d, gain, w_qkv, rope, seq):
    m, d = x2d.shape
    n = w_qkv.shape[1]
    bm = _block(seq, 512)
    bn = _block(d, 1024)
    n_pos_blocks = seq // bm
    cos, sa, sb = rope
    rope_spec = pl.BlockSpec((bm, HEAD_DIM), lambda i, j: (i % n_pos_blocks, 0))
    return pl.pallas_call(
        functools.partial(_qkv_kernel, n_q_blocks=d // bn, q_scale=HEAD_DIM ** -0.5 * math.log2(math.e)),
        grid=(m // bm, n // bn),
        in_specs=[
            pl.BlockSpec((bm, d), lambda i, j: (i, 0)),
            pl.BlockSpec((1, d), lambda i, j: (0, 0)),
            pl.BlockSpec((d, bn), lambda i, j: (0, j)),
            rope_spec, rope_spec, rope_spec,
        ],
        out_specs=pl.BlockSpec((bm, bn), lambda i, j: (i, j)),
        out_shape=jax.ShapeDtypeStruct((m, n), BF16),
        scratch_shapes=[pltpu.VMEM((bm, d), BF16)],
        compiler_params=_params("parallel", "arbitrary"),
        name="qkv",
    )(x2d, gain, w_qkv, cos, sa, sb)


def _rope_tables(seq):
    inv_freq = ROPE_THETA ** (-jnp.arange(0, ROT_DIM, 2, dtype=F32) / ROT_DIM)
    ang = jnp.arange(seq, dtype=F32)[:, None] * inv_freq[None, :]
    c, s = jnp.cos(ang), jnp.sin(ang)
    pad = HEAD_DIM - ROT_DIM
    cos = jnp.concatenate([c, c, jnp.ones((seq, pad), F32)], axis=1)
    sa = jnp.concatenate([-s, jnp.zeros((seq, pad + ROT_HALF), F32)], axis=1)
    sb = jnp.concatenate([jnp.zeros((seq, ROT_HALF), F32), s, jnp.zeros((seq, pad), F32)], axis=1)
    return cos, sa, sb


ATTN_ROWS = 128


def _attn_kernel(q_ref, k_ref, v_ref, lq1_ref, lk1_ref, lq2_ref, lk2_ref, sub_ref, o_ref, *, lambda_init):
    lam = (jnp.exp(jnp.sum(lq1_ref[...] * lk1_ref[...], axis=-1, keepdims=True))
           - jnp.exp(jnp.sum(lq2_ref[...] * lk2_ref[...], axis=-1, keepdims=True))
           + lambda_init)
    k = k_ref[...]
    v = v_ref[...]
    nt = (((1,), (1,)), ((), ()))
    rows = min(ATTN_ROWS, q_ref.shape[0])
    tasks = [(r, c) for r in range(q_ref.shape[0] // rows) for c in range(2)]

    def scores(t):
        r, c = tasks[t]
        return lax.dot_general(q_ref[r * rows:(r + 1) * rows, c * HEAD_DIM:(c + 1) * HEAD_DIM],
                               k[:, c * HEAD_DIM:(c + 1) * HEAD_DIM], nt, preferred_element_type=F32)

    def exponentials(s):
        p = jnp.exp2(s - jnp.max(s, axis=-1, keepdims=True))
        return p.astype(BF16), jnp.sum(p, axis=-1, keepdims=True)

    n = len(tasks)
    s = {t: scores(t) for t in range(min(2, n))}
    p = {0: exponentials(s.pop(0))}
    pv, norm = {}, {}
    for t in range(n):
        pb, norm[t] = p.pop(t)
        pv[t] = jnp.dot(pb, v, preferred_element_type=F32)
        if t + 2 < n:
            s[t + 2] = scores(t + 2)
        if t + 1 < n:
            p[t + 1] = exponentials(s.pop(t + 1))
        if tasks[t][1] == 1:
            r = tasks[t][0]
            o = pv.pop(t - 1) * (1.0 / norm[t - 1]) - pv.pop(t) * (lam / norm[t])
            o = _rmsnorm_rows(o, sub_ref[...], SUBLN_EPS) * (1.0 - lambda_init)
            o_ref[r * rows:(r + 1) * rows, :] = o.astype(o_ref.dtype)


def _diff_attention(qkv, lq1, lk1, lq2, lk2, subln, batch, seq, lambda_init):
    m, n3 = qkv.shape
    d = n3 // 3
    heads = d // V_DIM
    bq = _block(seq, 1024)
    nq = seq // bq
    vec = pl.BlockSpec((1, HEAD_DIM), lambda b, h, i: (0, 0))
    return pl.pallas_call(
        functools.partial(_attn_kernel, lambda_init=lambda_init),
        grid=(batch, heads, nq),
        in_specs=[
            pl.BlockSpec((bq, V_DIM), lambda b, h, i: (b * nq + i, h)),
            pl.BlockSpec((seq, V_DIM), lambda b, h, i: (b, heads + h)),
            pl.BlockSpec((seq, V_DIM), lambda b, h, i: (b, 2 * heads + h)),
            vec, vec, vec, vec,
            pl.BlockSpec((1, V_DIM), lambda b, h, i: (0, 0)),
        ],
        out_specs=pl.BlockSpec((bq, V_DIM), lambda b, h, i: (b * nq + i, h)),
        out_shape=jax.ShapeDtypeStruct((m, d), BF16),
        compiler_params=_params("parallel", "parallel", "arbitrary"),
        name="diff_attn",
    )(qkv, qkv, qkv, lq1, lk1, lq2, lk2, subln)


def _proj_kernel(a_ref, w_ref, res_ref, o_ref):
    o_ref[...] = res_ref[...] + jnp.dot(a_ref[...], w_ref[...], preferred_element_type=F32)


def _proj_residual(a, w, res):
    m, k = a.shape
    n = w.shape[1]
    bm = _block(m, 1024)
    bn = _block(n, 512)
    return pl.pallas_call(
        _proj_kernel,
        grid=(m // bm, n // bn),
        in_specs=[
            pl.BlockSpec((bm, k), lambda i, j: (i, 0)),
            pl.BlockSpec((k, bn), lambda i, j: (0, j)),
            pl.BlockSpec((bm, bn), lambda i, j: (i, j)),
        ],
        out_specs=pl.BlockSpec((bm, bn), lambda i, j: (i, j)),
        out_shape=jax.ShapeDtypeStruct((m, n), F32),
        compiler_params=_params("parallel", "arbitrary"),
        name="proj_residual",
    )(a, w, res)


def _ffn_kernel(x_hbm, g_ref, wgu_hbm, wd_hbm, gf_ref, o_ref, h_ref, wg_buf, wu_buf, wd_buf, x_sem, w_sem, *,
                layer, n_chunks, final_norm):
    bm = o_ref.shape[0]
    bc = wg_buf.shape[2]

    def weight_copies(j, slot):
        lo = pl.multiple_of(j * bc, bc)
        return (
            pltpu.make_async_copy(wgu_hbm.at[layer, :, pl.ds(lo, bc)], wg_buf.at[slot], w_sem.at[slot, 0]),
            pltpu.make_async_copy(wgu_hbm.at[layer, :, pl.ds(pl.multiple_of(n_chunks * bc + lo, bc), bc)],
                                  wu_buf.at[slot], w_sem.at[slot, 1]),
            pltpu.make_async_copy(wd_hbm.at[layer, pl.ds(lo, bc), :], wd_buf.at[slot], w_sem.at[slot, 2]),
        )

    for copy in weight_copies(0, 0):
        copy.start()
    rows = pl.ds(pl.multiple_of(pl.program_id(0) * bm, bm), bm)
    fetch = pltpu.make_async_copy(x_hbm.at[rows, :], o_ref, x_sem)
    fetch.start()
    fetch.wait()
    _rmsnorm_block(o_ref, g_ref, h_ref, EPS)

    def chunk(j, slot, prefetch_next):
        for copy in weight_copies(j, slot):
            copy.wait()
        if prefetch_next:
            for copy in weight_copies(j + 1, 1 - slot):
                copy.start()
        h = h_ref[...]
        gate = jnp.dot(h, wg_buf[slot], preferred_element_type=F32)
        up = jnp.dot(h, wu_buf[slot], preferred_element_type=F32)
        act = (gate / (1.0 + jnp.exp(-gate)) * up).astype(BF16)
        o_ref[...] += jnp.dot(act, wd_buf[slot], preferred_element_type=F32)

    def pair(jj, carry):
        chunk(2 * jj, 0, True)
        chunk(2 * jj + 1, 1, True)
        return carry

    n_pairs = (n_chunks - 1) // 2
    lax.fori_loop(0, n_pairs, pair, 0)
    for j in range(2 * n_pairs, n_chunks):
        chunk(j, j % 2, j + 1 < n_chunks)

    if final_norm:
        _rmsnorm_block(o_ref, gf_ref, o_ref, EPS)


def _ffn(x2d, gain, w_gate_up, w_down, layer, gain_final, final_norm):
    m, d = x2d.shape
    f = w_down.shape[1]
    bm = _block(m, 1024)
    bc = _block(f, 256)
    return pl.pallas_call(
        functools.partial(_ffn_kernel, layer=layer, n_chunks=f // bc, final_norm=final_norm),
        grid=(m // bm,),
        in_specs=[
            pl.BlockSpec(memory_space=pl.ANY),
            pl.BlockSpec((1, d), lambda i: (0, 0)),
            pl.BlockSpec(memory_space=pl.ANY),
            pl.BlockSpec(memory_space=pl.ANY),
            pl.BlockSpec((1, d), lambda i: (0, 0)),
        ],
        out_specs=pl.BlockSpec((bm, d), lambda i: (i, 0)),
        out_shape=jax.ShapeDtypeStruct((m, d), F32),
        scratch_shapes=[
            pltpu.VMEM((bm, d), BF16),
            pltpu.VMEM((2, d, bc), BF16), pltpu.VMEM((2, d, bc), BF16), pltpu.VMEM((2, bc, d), BF16),
            pltpu.SemaphoreType.DMA(()), pltpu.SemaphoreType.DMA((2, 3)),
        ],
        compiler_params=_params("arbitrary"),
        name="ffn",
    )(x2d, gain, w_gate_up, w_down, gain_final)


def _dft_matrices(n, scale):
    r = 1
    while r * r < n:
        r *= 2
    k = jnp.arange(n, dtype=jnp.int32)[:, None]
    hi = jnp.arange(n // r, dtype=jnp.int32)[None, :] * r
    lo = jnp.arange(r, dtype=jnp.int32)[None, :]
    w = 2.0 * math.pi / n
    a_hi = ((k * hi) % n).astype(F32) * w
    a_lo = ((k * lo) % n).astype(F32) * w
    ch, sh = jnp.cos(a_hi)[:, :, None], jnp.sin(a_hi)[:, :, None]
    cl, sl = jnp.cos(a_lo)[:, None, :], jnp.sin(a_lo)[:, None, :]
    c = (ch * cl - sh * sl).reshape(n, n) * scale
    s = (sh * cl + ch * sl).reshape(n, n) * scale
    return c.astype(BF16), s.astype(BF16)


RADIX = 4
DFT_ROWS = 128


def _twiddle_table(seq):
    n = jnp.arange(seq // RADIX, dtype=jnp.int32)[:, None]
    rho = jnp.arange(1, RADIX, dtype=jnp.int32)[None, :]
    ang = (n * rho).astype(F32) * (2.0 * math.pi / seq)
    cs = jnp.stack([jnp.cos(ang), jnp.sin(ang)], axis=-1).reshape(seq // RADIX, 2 * (RADIX - 1))
    return jnp.concatenate([cs, jnp.zeros((seq // RADIX, 2), F32)], axis=1)


def _dft_ch_kernel(x0_ref, x1_ref, x2_ref, x3_ref, g_ref, cs_ref, tw_ref, zr_ref, zi_ref,
                   sum_ref, alt_ref, d02_ref, d13_ref, *, group_dim):
    g = g_ref[...]

    def body(r, carry):
        rows = pl.ds(pl.multiple_of(r * NORM_ROWS, NORM_ROWS), NORM_ROWS)
        h0 = _rmsnorm_rows(x0_ref[rows, :], g, EPS)
        h1 = _rmsnorm_rows(x1_ref[rows, :], g, EPS)
        h2 = _rmsnorm_rows(x2_ref[rows, :], g, EPS)
        h3 = _rmsnorm_rows(x3_ref[rows, :], g, EPS)
        even, odd = h0 + h2, h1 + h3
        sum_ref[rows, :] = (even + odd).astype(BF16)
        alt_ref[rows, :] = (even - odd).astype(BF16)
        d02_ref[rows, :] = (h0 - h2).astype(BF16)
        d13_ref[rows, :] = (h1 - h3).astype(BF16)
        return carry

    lax.fori_loop(0, x0_ref.shape[0] // NORM_ROWS, body, 0)

    cs = cs_ref[...]
    tw = tw_ref[...]
    gd = group_dim
    for grp in range(sum_ref.shape[1] // gd):
        cols = slice(grp * gd, (grp + 1) * gd)
        p = jnp.dot(sum_ref[:, cols], cs, preferred_element_type=F32)
        m = jnp.dot(alt_ref[:, cols], cs, preferred_element_type=F32)
        c = jnp.dot(d02_ref[:, cols], cs, preferred_element_type=F32)
        d = jnp.dot(d13_ref[:, cols], cs, preferred_element_type=F32)
        u = [(p[:, :gd], -p[:, gd:]),
             (c[:, :gd] - d[:, gd:], -(c[:, gd:] + d[:, :gd])),
             (m[:, :gd], -m[:, gd:]),
             (c[:, :gd] + d[:, gd:], d[:, :gd] - c[:, gd:])]
        for rho, (ur, ui) in enumerate(u):
            if rho:
                ct, st = tw[:, 2 * rho - 2:2 * rho - 1], tw[:, 2 * rho - 1:2 * rho]
                ur, ui = ur * ct + ui * st, ui * ct - ur * st
            zr_ref[rho, :, cols] = ur.astype(zr_ref.dtype)
            zi_ref[rho, :, cols] = ui.astype(zi_ref.dtype)


def _dft_channels(x2d, gain, cs, tw, batch, seq):
    m, d = x2d.shape
    gd = cs.shape[0]
    quarter = seq // RADIX
    bm = _block(quarter, DFT_ROWS)
    ni = quarter // bm

    def x_spec(q):
        return pl.BlockSpec((bm, d), lambda b, i: (b * RADIX * ni + q * ni + i, 0))

    z_spec = pl.BlockSpec((RADIX, bm, d), lambda b, i: (0, b * ni + i, 0))
    return pl.pallas_call(
        functools.partial(_dft_ch_kernel, group_dim=gd),
        grid=(batch, ni),
        in_specs=[x_spec(0), x_spec(1), x_spec(2), x_spec(3),
                  pl.BlockSpec((1, d), lambda b, i: (0, 0)),
                  pl.BlockSpec(cs.shape, lambda b, i: (0, 0)),
                  pl.BlockSpec((bm, tw.shape[1]), lambda b, i: (i, 0))],
        out_specs=[z_spec, z_spec],
        out_shape=[jax.ShapeDtypeStruct((RADIX, m // RADIX, d), BF16)] * 2,
        scratch_shapes=[pltpu.VMEM((bm, d), BF16)] * 4,
        compiler_params=_params("parallel", "arbitrary"),
        name="dft_channels",
    )(x2d, x2d, x2d, x2d, gain, cs, tw)


def _dft_pos_kernel(c_ref, s_ref, zr_ref, zi_ref, o_ref, rows_ref):
    cmat, smat = c_ref[...], s_ref[...]
    quarter = cmat.shape[0]
    chunks = o_ref.shape[1] // LANES
    for rho in range(RADIX):
        y = (jnp.dot(cmat, zr_ref[rho], preferred_element_type=F32)
             + jnp.dot(smat, zi_ref[rho], preferred_element_type=F32))
        for c in range(chunks):
            rows_ref[c, pl.ds(rho, quarter, stride=RADIX), :] = y[:, c * LANES:(c + 1) * LANES]
    for c in range(chunks):
        o_ref[:, c * LANES:(c + 1) * LANES] = rows_ref[c].astype(o_ref.dtype)


def _dft_positions(cmat, smat, zr, zi, batch):
    _, mq, d = zr.shape
    quarter = cmat.shape[0]
    bn = _block(d, max(LANES, (512 * 1024) // quarter))
    mat = pl.BlockSpec((quarter, quarter), lambda b, j: (0, 0))
    col = pl.BlockSpec((RADIX, quarter, bn), lambda b, j: (0, b, j))
    return pl.pallas_call(
        _dft_pos_kernel,
        grid=(batch, d // bn),
        in_specs=[mat, mat, col, col],
        out_specs=pl.BlockSpec((RADIX * quarter, bn), lambda b, j: (b, j)),
        out_shape=jax.ShapeDtypeStruct((mq * RADIX, d), BF16),
        scratch_shapes=[pltpu.VMEM((bn // LANES, RADIX * quarter, LANES), F32)],
        compiler_params=_params("parallel", "arbitrary"),
        name="dft_positions",
    )(cmat, smat, zr, zi)


def _trunk(x, p):
    batch, seq, d = x.shape
    depth = p["norm_mix"].shape[0]
    x2d = x.reshape(batch * seq, d)
    rope = _rope_tables(seq)
    group_dim = d // N_FOURIER_GROUPS
    for i in range(depth):
        j = i // N_MIXERS
        gain_mix = p["norm_mix"][i][None, :]
        if i % N_MIXERS == 0:
            lambda_init = 0.8 - 0.6 * math.exp(-0.3 * i)
            qkv = _qkv_proj(x2d, gain_mix, p["w_qkv"][j], rope, seq)
            o = _diff_attention(qkv, p["lq1"][j][None, :], p["lk1"][j][None, :], p["lq2"][j][None, :],
                                p["lk2"][j][None, :], p["subln"][j][None, :], batch, seq, lambda_init)
            x2d = _proj_residual(o, p["w_o"][j], x2d)
        else:
            cc, sc = _dft_matrices(group_dim, group_dim ** -0.5)
            cq, sq = _dft_matrices(seq // RADIX, seq ** -0.5)
            zr, zi = _dft_channels(x2d, gain_mix, jnp.concatenate([cc, sc], axis=1), _twiddle_table(seq),
                                   batch, seq)
            f = _dft_positions(cq, sq, zr, zi, batch)
            x2d = _proj_residual(f, p["w_f"][j], x2d)
        last = i == depth - 1
        x2d = _ffn(x2d, p["norm_ffn"][i][None, :], p["w_gate_up"], p["w_down"], i,
                   p["norm_final"][None, :], final_norm=last)
    return x2d.reshape(batch, seq, d)


def kernel(x_prompt, x_sample, norm_mix, norm_ffn, norm_final, attn_w_qkv, attn_w_o, attn_lambda_q1,
           attn_lambda_k1, attn_lambda_q2, attn_lambda_k2, attn_subln, fourier_w_o, ffn_w_gate_up, ffn_w_down):
    assert x_prompt.shape[-1] % V_DIM == 0 and x_prompt.shape[-1] % (N_FOURIER_GROUPS * LANES) == 0
    p = {
        "norm_mix": norm_mix, "norm_ffn": norm_ffn, "norm_final": norm_final,
        "w_qkv": attn_w_qkv.astype(BF16), "w_o": attn_w_o.astype(BF16),
        "lq1": attn_lambda_q1, "lk1": attn_lambda_k1, "lq2": attn_lambda_q2, "lk2": attn_lambda_k2,
        "subln": attn_subln, "w_f": fourier_w_o.astype(BF16),
        "w_gate_up": ffn_w_gate_up.astype(BF16), "w_down": ffn_w_down.astype(BF16),
    }
    return (_trunk(x_prompt, p), _trunk(x_sample, p))
```

```python
import functools
import math

import numpy as np
import jax
import jax.numpy as jnp
from jax import lax
from jax.experimental import pallas as pl
from jax.experimental.pallas import tpu as pltpu

HEAD_DIM = 128
V_DIM = 2 * HEAD_DIM
ROT_DIM = HEAD_DIM // 4
ROT_HALF = ROT_DIM // 2
ROPE_THETA = 500000.0
N_FOURIER_GROUPS = 8
N_MIXERS = 2
EPS = 1e-6
SUBLN_EPS = 1e-5
LANES = 128
VMEM_LIMIT_BYTES = 58 * 1024 * 1024

BF16 = jnp.bfloat16
F32 = jnp.float32


def _block(n, want):
    b = min(n, want)
    while n % b:
        b //= 2
    return b


def _params(*sem):
    return pltpu.CompilerParams(dimension_semantics=sem, vmem_limit_bytes=VMEM_LIMIT_BYTES)


def _rmsnorm_rows(x, g, eps):
    return x * lax.rsqrt(jnp.mean(x * x, axis=-1, keepdims=True) + eps) * g


NORM_ROWS = 16
NORM_UNROLL = 4


def _rmsnorm_block(x_ref, g_ref, o_ref, eps):
    g = g_ref[...]
    step = NORM_ROWS * NORM_UNROLL

    def body(r, carry):
        base = r * step
        groups = [pl.ds(pl.multiple_of(base + u * NORM_ROWS, NORM_ROWS), NORM_ROWS) for u in range(NORM_UNROLL)]
        xs = [x_ref[rows, :] for rows in groups]
        for rows, x in zip(groups, xs):
            o_ref[rows, :] = _rmsnorm_rows(x, g, eps).astype(o_ref.dtype)
        return carry

    lax.fori_loop(0, x_ref.shape[0] // step, body, 0)


ROPE_COLS = 512


def _qkv_kernel(x_ref, g_ref, w_ref, cos_ref, sa_ref, sb_ref, o_ref, h_ref, *, n_q_blocks, q_scale):
    j = pl.program_id(1)

    @pl.when(j == 0)
    def _():
        _rmsnorm_block(x_ref, g_ref, h_ref, EPS)

    def project(lo=0, width=None):
        width = w_ref.shape[1] if width is None else width
        return jnp.dot(h_ref[...], w_ref[:, lo:lo + width], preferred_element_type=F32)

    def rotary(scale):
        cos, sa, sb = cos_ref[...] * scale, sa_ref[...] * scale, sb_ref[...] * scale
        width = min(ROPE_COLS, w_ref.shape[1])
        for lo in range(0, w_ref.shape[1], width):
            acc = project(lo, width)
            for c in range(lo, lo + width, HEAD_DIM):
                t = acc[:, c - lo:c - lo + HEAD_DIM]
                r = t * cos + pltpu.roll(t, HEAD_DIM - ROT_HALF, 1) * sa + pltpu.roll(t, ROT_HALF, 1) * sb
                o_ref[:, c:c + HEAD_DIM] = r.astype(o_ref.dtype)

    @pl.when(j < n_q_blocks)
    def _():
        rotary(q_scale)

    @pl.when(jnp.logical_and(j >= n_q_blocks, j < 2 * n_q_blocks))
    def _():
        rotary(1.0)

    @pl.when(j >= 2 * n_q_blocks)
    def _():
        o_ref[...] = project().astype(o_ref.dtype)


QKV_COLS = 1024
PROJ_COLS = 512


def _qkv_proj(x2d, gain, w_qkv, rope, seq):
    m, d = x2d.shape
    bn = w_qkv.shape[2]
    n = w_qkv.shape[0] * bn
    bm = _block(seq, 512)
    n_pos_blocks = seq // bm
    cos, sa, sb = rope
    rope_spec = pl.BlockSpec((bm, HEAD_DIM), lambda i, j: (i % n_pos_blocks, 0))
    return pl.pallas_call(
        functools.partial(_qkv_kernel, n_q_blocks=d // bn, q_scale=HEAD_DIM ** -0.5 * math.log2(math.e)),
        grid=(m // bm, n // bn),
        in_specs=[
            pl.BlockSpec((bm, d), lambda i, j: (i, 0)),
            pl.BlockSpec((1, d), lambda i, j: (0, 0)),
            pl.BlockSpec((None, d, bn), lambda i, j: (j, 0, 0)),
            rope_spec, rope_spec, rope_spec,
        ],
        out_specs=pl.BlockSpec((bm, bn), lambda i, j: (i, j)),
        out_shape=jax.ShapeDtypeStruct((m, n), BF16),
        scratch_shapes=[pltpu.VMEM((bm, d), BF16)],
        compiler_params=_params("parallel", "arbitrary"),
        name="qkv",
    )(x2d, gain, w_qkv, cos, sa, sb)


def _rope_tables(seq):
    inv_freq = ROPE_THETA ** (-jnp.arange(0, ROT_DIM, 2, dtype=F32) / ROT_DIM)
    ang = jnp.arange(seq, dtype=F32)[:, None] * inv_freq[None, :]
    c, s = jnp.cos(ang), jnp.sin(ang)
    pad = HEAD_DIM - ROT_DIM
    cos = jnp.concatenate([c, c, jnp.ones((seq, pad), F32)], axis=1)
    sa = jnp.concatenate([-s, jnp.zeros((seq, pad + ROT_HALF), F32)], axis=1)
    sb = jnp.concatenate([jnp.zeros((seq, ROT_HALF), F32), s, jnp.zeros((seq, pad), F32)], axis=1)
    return cos, sa, sb


ATTN_ROWS = 128


def _attn_kernel(q_ref, k_ref, v_ref, lq1_ref, lk1_ref, lq2_ref, lk2_ref, sub_ref, o_ref, *, lambda_init):
    lam = (jnp.exp(jnp.sum(lq1_ref[...] * lk1_ref[...], axis=-1, keepdims=True))
           - jnp.exp(jnp.sum(lq2_ref[...] * lk2_ref[...], axis=-1, keepdims=True))
           + lambda_init)
    k = k_ref[...]
    v = v_ref[...]
    nt = (((1,), (1,)), ((), ()))
    rows = min(ATTN_ROWS, q_ref.shape[0])
    tasks = [(r, c) for r in range(q_ref.shape[0] // rows) for c in range(2)]

    def scores(t):
        r, c = tasks[t]
        return lax.dot_general(q_ref[r * rows:(r + 1) * rows, c * HEAD_DIM:(c + 1) * HEAD_DIM],
                               k[:, c * HEAD_DIM:(c + 1) * HEAD_DIM], nt, preferred_element_type=F32)

    def exponentials(s):
        p = jnp.exp2(s - jnp.max(s, axis=-1, keepdims=True))
        return p.astype(BF16), jnp.sum(p, axis=-1, keepdims=True)

    n = len(tasks)
    s = {t: scores(t) for t in range(min(2, n))}
    p = {0: exponentials(s.pop(0))}
    pv, norm = {}, {}
    for t in range(n):
        pb, norm[t] = p.pop(t)
        pv[t] = jnp.dot(pb, v, preferred_element_type=F32)
        if t + 2 < n:
            s[t + 2] = scores(t + 2)
        if t + 1 < n:
            p[t + 1] = exponentials(s.pop(t + 1))
        if tasks[t][1] == 1:
            r = tasks[t][0]
            o = pv.pop(t - 1) * (1.0 / norm[t - 1]) - pv.pop(t) * (lam / norm[t])
            o = _rmsnorm_rows(o, sub_ref[...], SUBLN_EPS) * (1.0 - lambda_init)
            o_ref[r * rows:(r + 1) * rows, :] = o.astype(o_ref.dtype)


def _diff_attention(qkv, lq1, lk1, lq2, lk2, subln, batch, seq, lambda_init):
    m, n3 = qkv.shape
    d = n3 // 3
    heads = d // V_DIM
    bq = _block(seq, 1024)
    nq = seq // bq
    vec = pl.BlockSpec((1, HEAD_DIM), lambda b, h, i: (0, 0))
    return pl.pallas_call(
        functools.partial(_attn_kernel, lambda_init=lambda_init),
        grid=(batch, heads, nq),
        in_specs=[
            pl.BlockSpec((bq, V_DIM), lambda b, h, i: (b * nq + i, h)),
            pl.BlockSpec((seq, V_DIM), lambda b, h, i: (b, heads + h)),
            pl.BlockSpec((seq, V_DIM), lambda b, h, i: (b, 2 * heads + h)),
            vec, vec, vec, vec,
            pl.BlockSpec((1, V_DIM), lambda b, h, i: (0, 0)),
        ],
        out_specs=pl.BlockSpec((bq, V_DIM), lambda b, h, i: (b * nq + i, h)),
        out_shape=jax.ShapeDtypeStruct((m, d), BF16),
        compiler_params=_params("parallel", "parallel", "arbitrary"),
        name="diff_attn",
    )(qkv, qkv, qkv, lq1, lk1, lq2, lk2, subln)


def _proj_kernel(a_ref, w_ref, res_ref, o_ref):
    o_ref[...] = res_ref[...] + jnp.dot(a_ref[...], w_ref[...], preferred_element_type=F32)


def _proj_residual(a, w, res):
    m, k = a.shape
    bn = w.shape[2]
    n = w.shape[0] * bn
    bm = _block(m, 1024)
    return pl.pallas_call(
        _proj_kernel,
        grid=(m // bm, n // bn),
        in_specs=[
            pl.BlockSpec((bm, k), lambda i, j: (i, 0)),
            pl.BlockSpec((None, k, bn), lambda i, j: (j, 0, 0)),
            pl.BlockSpec((bm, bn), lambda i, j: (i, j)),
        ],
        out_specs=pl.BlockSpec((bm, bn), lambda i, j: (i, j)),
        out_shape=jax.ShapeDtypeStruct((m, n), F32),
        compiler_params=_params("parallel", "arbitrary"),
        name="proj_residual",
    )(a, w, res)


FFN_CHUNK = 256


def _chunk_major(w, bc):
    layers, d, n = w.shape
    return w.reshape(layers, d, n // bc, bc).transpose(0, 2, 1, 3)


def _ffn_kernel(x_hbm, g_ref, wgu_hbm, wd_hbm, gf_ref, o_ref, h_ref, wg_buf, wu_buf, wd_buf, x_sem, w_sem, *,
                layer, n_chunks, final_norm):
    bm = o_ref.shape[0]
    bc = wg_buf.shape[2]

    def weight_copies(j, slot):
        lo = pl.multiple_of(j * bc, bc)
        return (
            pltpu.make_async_copy(wgu_hbm.at[layer, j], wg_buf.at[slot], w_sem.at[slot, 0]),
            pltpu.make_async_copy(wgu_hbm.at[layer, n_chunks + j], wu_buf.at[slot], w_sem.at[slot, 1]),
            pltpu.make_async_copy(wd_hbm.at[layer, pl.ds(lo, bc), :], wd_buf.at[slot], w_sem.at[slot, 2]),
        )

    for copy in weight_copies(0, 0):
        copy.start()
    rows = pl.ds(pl.multiple_of(pl.program_id(0) * bm, bm), bm)
    fetch = pltpu.make_async_copy(x_hbm.at[rows, :], o_ref, x_sem)
    fetch.start()
    fetch.wait()
    _rmsnorm_block(o_ref, g_ref, h_ref, EPS)

    def chunk(j, slot, prefetch_next):
        for copy in weight_copies(j, slot):
            copy.wait()
        if prefetch_next:
            for copy in weight_copies(j + 1, 1 - slot):
                copy.start()
        h = h_ref[...]
        gate = jnp.dot(h, wg_buf[slot], preferred_element_type=F32)
        up = jnp.dot(h, wu_buf[slot], preferred_element_type=F32)
        act = (gate / (1.0 + jnp.exp(-gate)) * up).astype(BF16)
        o_ref[...] += jnp.dot(act, wd_buf[slot], preferred_element_type=F32)

    def pair(jj, carry):
        chunk(2 * jj, 0, True)
        chunk(2 * jj + 1, 1, True)
        return carry

    n_pairs = (n_chunks - 1) // 2
    lax.fori_loop(0, n_pairs, pair, 0)
    for j in range(2 * n_pairs, n_chunks):
        chunk(j, j % 2, j + 1 < n_chunks)

    if final_norm:
        _rmsnorm_block(o_ref, gf_ref, o_ref, EPS)


def _ffn(x2d, gain, w_gate_up, w_down, layer, gain_final, final_norm):
    m, d = x2d.shape
    f = w_down.shape[1]
    bm = _block(m, 1024)
    bc = w_gate_up.shape[3]
    return pl.pallas_call(
        functools.partial(_ffn_kernel, layer=layer, n_chunks=f // bc, final_norm=final_norm),
        grid=(m // bm,),
        in_specs=[
            pl.BlockSpec(memory_space=pl.ANY),
            pl.BlockSpec((1, d), lambda i: (0, 0)),
            pl.BlockSpec(memory_space=pl.ANY),
            pl.BlockSpec(memory_space=pl.ANY),
            pl.BlockSpec((1, d), lambda i: (0, 0)),
        ],
        out_specs=pl.BlockSpec((bm, d), lambda i: (i, 0)),
        out_shape=jax.ShapeDtypeStruct((m, d), F32),
        scratch_shapes=[
            pltpu.VMEM((bm, d), BF16),
            pltpu.VMEM((2, d, bc), BF16), pltpu.VMEM((2, d, bc), BF16), pltpu.VMEM((2, bc, d), BF16),
            pltpu.SemaphoreType.DMA(()), pltpu.SemaphoreType.DMA((2, 3)),
        ],
        compiler_params=_params("arbitrary"),
        name="ffn",
    )(x2d, gain, w_gate_up, w_down, gain_final)


def _dft_matrices(n, scale):
    r = 1
    while r * r < n:
        r *= 2
    k = jnp.arange(n, dtype=jnp.int32)[:, None]
    hi = jnp.arange(n // r, dtype=jnp.int32)[None, :] * r
    lo = jnp.arange(r, dtype=jnp.int32)[None, :]
    w = 2.0 * math.pi / n
    a_hi = ((k * hi) % n).astype(F32) * w
    a_lo = ((k * lo) % n).astype(F32) * w
    ch, sh = jnp.cos(a_hi)[:, :, None], jnp.sin(a_hi)[:, :, None]
    cl, sl = jnp.cos(a_lo)[:, None, :], jnp.sin(a_lo)[:, None, :]
    c = (ch * cl - sh * sl).reshape(n, n) * scale
    s = (sh * cl + ch * sl).reshape(n, n) * scale
    return c.astype(BF16), s.astype(BF16)


RADIX = 4
DFT_ROWS = 128


def _twiddle_table(seq):
    n = jnp.arange(seq // RADIX, dtype=jnp.int32)[:, None]
    rho = jnp.arange(1, RADIX, dtype=jnp.int32)[None, :]
    ang = (n * rho).astype(F32) * (2.0 * math.pi / seq)
    cs = jnp.stack([jnp.cos(ang), jnp.sin(ang)], axis=-1).reshape(seq // RADIX, 2 * (RADIX - 1))
    return jnp.concatenate([cs, jnp.zeros((seq // RADIX, 2), F32)], axis=1)


def _dft_ch_kernel(x0_ref, x1_ref, x2_ref, x3_ref, g_ref, cs_ref, tw_ref, zr_ref, zi_ref,
                   sum_ref, alt_ref, d02_ref, d13_ref, *, group_dim):
    g = g_ref[...]

    def body(r, carry):
        rows = pl.ds(pl.multiple_of(r * NORM_ROWS, NORM_ROWS), NORM_ROWS)
        h0 = _rmsnorm_rows(x0_ref[rows, :], g, EPS)
        h1 = _rmsnorm_rows(x1_ref[rows, :], g, EPS)
        h2 = _rmsnorm_rows(x2_ref[rows, :], g, EPS)
        h3 = _rmsnorm_rows(x3_ref[rows, :], g, EPS)
        even, odd = h0 + h2, h1 + h3
        sum_ref[rows, :] = (even + odd).astype(BF16)
        alt_ref[rows, :] = (even - odd).astype(BF16)
        d02_ref[rows, :] = (h0 - h2).astype(BF16)
        d13_ref[rows, :] = (h1 - h3).astype(BF16)
        return carry

    lax.fori_loop(0, x0_ref.shape[0] // NORM_ROWS, body, 0)

    cs = cs_ref[...]
    tw = tw_ref[...]
    gd = group_dim
    for grp in range(sum_ref.shape[1] // gd):
        cols = slice(grp * gd, (grp + 1) * gd)
        p = jnp.dot(sum_ref[:, cols], cs, preferred_element_type=F32)
        m = jnp.dot(alt_ref[:, cols], cs, preferred_element_type=F32)
        c = jnp.dot(d02_ref[:, cols], cs, preferred_element_type=F32)
        d = jnp.dot(d13_ref[:, cols], cs, preferred_element_type=F32)
        u = [(p[:, :gd], -p[:, gd:]),
             (c[:, :gd] - d[:, gd:], -(c[:, gd:] + d[:, :gd])),
             (m[:, :gd], -m[:, gd:]),
             (c[:, :gd] + d[:, gd:], d[:, :gd] - c[:, gd:])]
        for rho, (ur, ui) in enumerate(u):
            if rho:
                ct, st = tw[:, 2 * rho - 2:2 * rho - 1], tw[:, 2 * rho - 1:2 * rho]
                ur, ui = ur * ct + ui * st, ui * ct - ur * st
            zr_ref[rho, :, cols] = ur.astype(zr_ref.dtype)
            zi_ref[rho, :, cols] = ui.astype(zi_ref.dtype)


def _dft_channels(x2d, gain, cs, tw, batch, seq):
    m, d = x2d.shape
    gd = cs.shape[0]
    quarter = seq // RADIX
    bm = _block(quarter, DFT_ROWS)
    ni = quarter // bm

    def x_spec(q):
        return pl.BlockSpec((bm, d), lambda b, i: (b * RADIX * ni + q * ni + i, 0))

    z_spec = pl.BlockSpec((RADIX, bm, d), lambda b, i: (0, b * ni + i, 0))
    return pl.pallas_call(
        functools.partial(_dft_ch_kernel, group_dim=gd),
        grid=(batch, ni),
        in_specs=[x_spec(0), x_spec(1), x_spec(2), x_spec(3),
                  pl.BlockSpec((1, d), lambda b, i: (0, 0)),
                  pl.BlockSpec(cs.shape, lambda b, i: (0, 0)),
                  pl.BlockSpec((bm, tw.shape[1]), lambda b, i: (i, 0))],
        out_specs=[z_spec, z_spec],
        out_shape=[jax.ShapeDtypeStruct((RADIX, m // RADIX, d), BF16)] * 2,
        scratch_shapes=[pltpu.VMEM((bm, d), BF16)] * 4,
        compiler_params=_params("parallel", "arbitrary"),
        name="dft_channels",
    )(x2d, x2d, x2d, x2d, gain, cs, tw)


def _dft_pos_kernel(c_ref, s_ref, zr_ref, zi_ref, o_ref, rows_ref):
    cmat, smat = c_ref[...], s_ref[...]
    quarter = cmat.shape[0]
    chunks = o_ref.shape[1] // LANES
    for rho in range(RADIX):
        y = (jnp.dot(cmat, zr_ref[rho], preferred_element_type=F32)
             + jnp.dot(smat, zi_ref[rho], preferred_element_type=F32))
        for c in range(chunks):
            rows_ref[c, pl.ds(rho, quarter, stride=RADIX), :] = y[:, c * LANES:(c + 1) * LANES]
    for c in range(chunks):
        o_ref[:, c * LANES:(c + 1) * LANES] = rows_ref[c].astype(o_ref.dtype)


def _dft_positions(cmat, smat, zr, zi, batch):
    _, mq, d = zr.shape
    quarter = cmat.shape[0]
    bn = _block(d, max(LANES, (512 * 1024) // quarter))
    mat = pl.BlockSpec((quarter, quarter), lambda b, j: (0, 0))
    col = pl.BlockSpec((RADIX, quarter, bn), lambda b, j: (0, b, j))
    return pl.pallas_call(
        _dft_pos_kernel,
        grid=(batch, d // bn),
        in_specs=[mat, mat, col, col],
        out_specs=pl.BlockSpec((RADIX * quarter, bn), lambda b, j: (b, j)),
        out_shape=jax.ShapeDtypeStruct((mq * RADIX, d), BF16),
        scratch_shapes=[pltpu.VMEM((bn // LANES, RADIX * quarter, LANES), F32)],
        compiler_params=_params("parallel", "arbitrary"),
        name="dft_positions",
    )(cmat, smat, zr, zi)


def _trunk(x, p):
    batch, seq, d = x.shape
    depth = p["norm_mix"].shape[0]
    x2d = x.reshape(batch * seq, d)
    rope = _rope_tables(seq)
    group_dim = d // N_FOURIER_GROUPS
    for i in range(depth):
        j = i // N_MIXERS
        gain_mix = p["norm_mix"][i][None, :]
        if i % N_MIXERS == 0:
            lambda_init = 0.8 - 0.6 * math.exp(-0.3 * i)
            qkv = _qkv_proj(x2d, gain_mix, p["w_qkv"][j], rope, seq)
            o = _diff_attention(qkv, p["lq1"][j][None, :], p["lk1"][j][None, :], p["lq2"][j][None, :],
                                p["lk2"][j][None, :], p["subln"][j][None, :], batch, seq, lambda_init)
            x2d = _proj_residual(o, p["w_o"][j], x2d)
        else:
            cc, sc = _dft_matrices(group_dim, group_dim ** -0.5)
            cq, sq = _dft_matrices(seq // RADIX, seq ** -0.5)
            zr, zi = _dft_channels(x2d, gain_mix, jnp.concatenate([cc, sc], axis=1), _twiddle_table(seq),
                                   batch, seq)
            f = _dft_positions(cq, sq, zr, zi, batch)
            x2d = _proj_residual(f, p["w_f"][j], x2d)
        last = i == depth - 1
        x2d = _ffn(x2d, p["norm_ffn"][i][None, :], p["w_gate_up"], p["w_down"], i,
                   p["norm_final"][None, :], final_norm=last)
    return x2d.reshape(batch, seq, d)


def kernel(x_prompt, x_sample, norm_mix, norm_ffn, norm_final, attn_w_qkv, attn_w_o, attn_lambda_q1,
           attn_lambda_k1, attn_lambda_q2, attn_lambda_k2, attn_subln, fourier_w_o, ffn_w_gate_up, ffn_w_down):
    d_model = x_prompt.shape[-1]
    assert d_model % V_DIM == 0 and d_model % (N_FOURIER_GROUPS * LANES) == 0
    p = {
        "norm_mix": norm_mix, "norm_ffn": norm_ffn, "norm_final": norm_final,
        "w_qkv": _chunk_major(attn_w_qkv.astype(BF16), _block(d_model, QKV_COLS)),
        "w_o": _chunk_major(attn_w_o.astype(BF16), _block(d_model, PROJ_COLS)),
        "lq1": attn_lambda_q1, "lk1": attn_lambda_k1, "lq2": attn_lambda_q2, "lk2": attn_lambda_k2,
        "subln": attn_subln, "w_f": _chunk_major(fourier_w_o.astype(BF16), _block(d_model, PROJ_COLS)),
        "w_gate_up": _chunk_major(ffn_w_gate_up.astype(BF16), _block(ffn_w_down.shape[1], FFN_CHUNK)),
        "w_down": ffn_w_down.astype(BF16),
    }
    return (_trunk(x_prompt, p), _trunk(x_sample, p))
```

```python
import functools
import math

import numpy as np
import jax
import jax.numpy as jnp
from jax import lax
from jax.experimental import pallas as pl
from jax.experimental.pallas import tpu as pltpu

HEAD_DIM = 128
V_DIM = 2 * HEAD_DIM
ROT_DIM = HEAD_DIM // 4
ROT_HALF = ROT_DIM // 2
ROPE_THETA = 500000.0
N_FOURIER_GROUPS = 8
N_MIXERS = 2
EPS = 1e-6
SUBLN_EPS = 1e-5
LANES = 128
VMEM_LIMIT_BYTES = 58 * 1024 * 1024

BF16 = jnp.bfloat16
F32 = jnp.float32


def _block(n, want):
    b = min(n, want)
    while n % b:
        b //= 2
    return b


def _params(*sem):
    return pltpu.CompilerParams(dimension_semantics=sem, vmem_limit_bytes=VMEM_LIMIT_BYTES)


def _rmsnorm_rows(x, g, eps):
    return x * lax.rsqrt(jnp.mean(x * x, axis=-1, keepdims=True) + eps) * g


NORM_ROWS = 16
NORM_UNROLL = 4


def _rmsnorm_block(x_ref, g_ref, o_ref, eps):
    g = g_ref[...]
    step = NORM_ROWS * NORM_UNROLL

    def body(r, carry):
        base = r * step
        groups = [pl.ds(pl.multiple_of(base + u * NORM_ROWS, NORM_ROWS), NORM_ROWS) for u in range(NORM_UNROLL)]
        xs = [x_ref[rows, :] for rows in groups]
        for rows, x in zip(groups, xs):
            o_ref[rows, :] = _rmsnorm_rows(x, g, eps).astype(o_ref.dtype)
        return carry

    lax.fori_loop(0, x_ref.shape[0] // step, body, 0)


ROPE_COLS = 512


def _qkv_kernel(x_ref, g_ref, w_ref, cos_ref, sa_ref, sb_ref, o_ref, h_ref, *, n_q_blocks, q_scale):
    j = pl.program_id(1)

    @pl.when(j == 0)
    def _():
        _rmsnorm_block(x_ref, g_ref, h_ref, EPS)

    def project(lo=0, width=None):
        width = w_ref.shape[1] if width is None else width
        return jnp.dot(h_ref[...], w_ref[:, lo:lo + width], preferred_element_type=F32)

    def rotary(scale):
        cos, sa, sb = cos_ref[...] * scale, sa_ref[...] * scale, sb_ref[...] * scale
        width = min(ROPE_COLS, w_ref.shape[1])
        for lo in range(0, w_ref.shape[1], width):
            acc = project(lo, width)
            for c in range(lo, lo + width, HEAD_DIM):
                t = acc[:, c - lo:c - lo + HEAD_DIM]
                r = t * cos + pltpu.roll(t, HEAD_DIM - ROT_HALF, 1) * sa + pltpu.roll(t, ROT_HALF, 1) * sb
                o_ref[:, c:c + HEAD_DIM] = r.astype(o_ref.dtype)

    @pl.when(j < n_q_blocks)
    def _():
        rotary(q_scale)

    @pl.when(jnp.logical_and(j >= n_q_blocks, j < 2 * n_q_blocks))
    def _():
        rotary(1.0)

    @pl.when(j >= 2 * n_q_blocks)
    def _():
        o_ref[...] = project().astype(o_ref.dtype)


def _qkv_proj(x2d, gain, w_qkv, rope, seq):
    m, d = x2d.shape
    n = w_qkv.shape[1]
    bm = _block(seq, 512)
    bn = _block(d, 1024)
    n_pos_blocks = seq // bm
    cos, sa, sb = rope
    rope_spec = pl.BlockSpec((bm, HEAD_DIM), lambda i, j: (i % n_pos_blocks, 0))
    return pl.pallas_call(
        functools.partial(_qkv_kernel, n_q_blocks=d // bn, q_scale=HEAD_DIM ** -0.5 * math.log2(math.e)),
        grid=(m // bm, n // bn),
        in_specs=[
            pl.BlockSpec((bm, d), lambda i, j: (i, 0)),
            pl.BlockSpec((1, d), lambda i, j: (0, 0)),
            pl.BlockSpec((d, bn), lambda i, j: (0, j)),
            rope_spec, rope_spec, rope_spec,
        ],
        out_specs=pl.BlockSpec((bm, bn), lambda i, j: (i, j)),
        out_shape=jax.ShapeDtypeStruct((m, n), BF16),
        scratch_shapes=[pltpu.VMEM((bm, d), BF16)],
        compiler_params=_params("parallel", "arbitrary"),
        name="qkv",
    )(x2d, gain, w_qkv, cos, sa, sb)


def _rope_tables(seq):
    inv_freq = ROPE_THETA ** (-jnp.arange(0, ROT_DIM, 2, dtype=F32) / ROT_DIM)
    ang = jnp.arange(seq, dtype=F32)[:, None] * inv_freq[None, :]
    c, s = jnp.cos(ang), jnp.sin(ang)
    pad = HEAD_DIM - ROT_DIM
    cos = jnp.concatenate([c, c, jnp.ones((seq, pad), F32)], axis=1)
    sa = jnp.concatenate([-s, jnp.zeros((seq, pad + ROT_HALF), F32)], axis=1)
    sb = jnp.concatenate([jnp.zeros((seq, ROT_HALF), F32), s, jnp.zeros((seq, pad), F32)], axis=1)
    return cos, sa, sb


ATTN_ROWS = 128


def _row_groups(n_rows):
    if n_rows < 4 * ATTN_ROWS:
        rows = min(ATTN_ROWS, n_rows)
        sizes = [rows] * (n_rows // rows)
    else:
        edge = ATTN_ROWS // 2
        sizes = [edge, edge] + [ATTN_ROWS] * ((n_rows - 4 * edge) // ATTN_ROWS) + [edge, edge]
    assert sum(sizes) == n_rows
    starts = [sum(sizes[:i]) for i in range(len(sizes))]
    return list(zip(starts, sizes))


def _attn_kernel(q_ref, k_ref, v_ref, lq1_ref, lk1_ref, lq2_ref, lk2_ref, sub_ref, o_ref, *, lambda_init):
    lam = (jnp.exp(jnp.sum(lq1_ref[...] * lk1_ref[...], axis=-1, keepdims=True))
           - jnp.exp(jnp.sum(lq2_ref[...] * lk2_ref[...], axis=-1, keepdims=True))
           + lambda_init)
    k = k_ref[...]
    v = v_ref[...]
    nt = (((1,), (1,)), ((), ()))
    tasks = [(r0, rows, c) for r0, rows in _row_groups(q_ref.shape[0]) for c in range(2)]

    def scores(t):
        r0, rows, c = tasks[t]
        return lax.dot_general(q_ref[r0:r0 + rows, c * HEAD_DIM:(c + 1) * HEAD_DIM],
                               k[:, c * HEAD_DIM:(c + 1) * HEAD_DIM], nt, preferred_element_type=F32)

    def exponentials(s):
        p = jnp.exp2(s - jnp.max(s, axis=-1, keepdims=True))
        return p.astype(BF16), jnp.sum(p, axis=-1, keepdims=True)

    n = len(tasks)
    s = {t: scores(t) for t in range(min(2, n))}
    p = {0: exponentials(s.pop(0))}
    pv, norm = {}, {}
    for t in range(n):
        pb, norm[t] = p.pop(t)
        pv[t] = jnp.dot(pb, v, preferred_element_type=F32)
        if t + 2 < n:
            s[t + 2] = scores(t + 2)
        if t + 1 < n:
            p[t + 1] = exponentials(s.pop(t + 1))
        if tasks[t][2] == 1:
            r0, rows, _ = tasks[t]
            o = pv.pop(t - 1) * (1.0 / norm[t - 1]) - pv.pop(t) * (lam / norm[t])
            o = _rmsnorm_rows(o, sub_ref[...], SUBLN_EPS) * (1.0 - lambda_init)
            o_ref[r0:r0 + rows, :] = o.astype(o_ref.dtype)


def _diff_attention(qkv, lq1, lk1, lq2, lk2, subln, batch, seq, lambda_init):
    m, n3 = qkv.shape
    d = n3 // 3
    heads = d // V_DIM
    bq = _block(seq, 1024)
    nq = seq // bq
    vec = pl.BlockSpec((1, HEAD_DIM), lambda b, h, i: (0, 0))
    return pl.pallas_call(
        functools.partial(_attn_kernel, lambda_init=lambda_init),
        grid=(batch, heads, nq),
        in_specs=[
            pl.BlockSpec((bq, V_DIM), lambda b, h, i: (b * nq + i, h)),
            pl.BlockSpec((seq, V_DIM), lambda b, h, i: (b, heads + h)),
            pl.BlockSpec((seq, V_DIM), lambda b, h, i: (b, 2 * heads + h)),
            vec, vec, vec, vec,
            pl.BlockSpec((1, V_DIM), lambda b, h, i: (0, 0)),
        ],
        out_specs=pl.BlockSpec((bq, V_DIM), lambda b, h, i: (b * nq + i, h)),
        out_shape=jax.ShapeDtypeStruct((m, d), BF16),
        compiler_params=_params("parallel", "parallel", "arbitrary"),
        name="diff_attn",
    )(qkv, qkv, qkv, lq1, lk1, lq2, lk2, subln)


def _proj_kernel(a_ref, w_ref, res_ref, o_ref):
    o_ref[...] = res_ref[...] + jnp.dot(a_ref[...], w_ref[...], preferred_element_type=F32)


def _proj_residual(a, w, res):
    m, k = a.shape
    n = w.shape[1]
    bm = _block(m, 1024)
    bn = _block(n, 1024)
    return pl.pallas_call(
        _proj_kernel,
        grid=(m // bm, n // bn),
        in_specs=[
            pl.BlockSpec((bm, k), lambda i, j: (i, 0)),
            pl.BlockSpec((k, bn), lambda i, j: (0, j)),
            pl.BlockSpec((bm, bn), lambda i, j: (i, j)),
        ],
        out_specs=pl.BlockSpec((bm, bn), lambda i, j: (i, j)),
        out_shape=jax.ShapeDtypeStruct((m, n), F32),
        compiler_params=_params("parallel", "arbitrary"),
        name="proj_residual",
    )(a, w, res)


def _ffn_kernel(x_hbm, g_ref, wgu_hbm, wd_hbm, gf_ref, o_ref, h_ref, wg_buf, wu_buf, wd_buf, x_sem, w_sem, *,
                layer, n_chunks, final_norm):
    bm = o_ref.shape[0]
    bc = wg_buf.shape[2]

    def weight_copies(j, slot):
        lo = pl.multiple_of(j * bc, bc)
        return (
            pltpu.make_async_copy(wgu_hbm.at[layer, :, pl.ds(lo, bc)], wg_buf.at[slot], w_sem.at[slot, 0]),
            pltpu.make_async_copy(wgu_hbm.at[layer, :, pl.ds(pl.multiple_of(n_chunks * bc + lo, bc), bc)],
                                  wu_buf.at[slot], w_sem.at[slot, 1]),
            pltpu.make_async_copy(wd_hbm.at[layer, pl.ds(lo, bc), :], wd_buf.at[slot], w_sem.at[slot, 2]),
        )

    for copy in weight_copies(0, 0):
        copy.start()
    half = bm // 2
    fetches = []
    for part in range(2):
        src = x_hbm.at[pl.ds(pl.multiple_of(pl.program_id(0) * bm + part * half, half), half), :]
        fetches.append(pltpu.make_async_copy(src, o_ref.at[pl.ds(part * half, half), :], x_sem.at[part]))
        fetches[part].start()
    for part in range(2):
        fetches[part].wait()
        part_rows = pl.ds(part * half, half)
        _rmsnorm_block(o_ref.at[part_rows, :], g_ref, h_ref.at[part_rows, :], EPS)

    def chunk(j, slot, prefetch_next):
        for copy in weight_copies(j, slot):
            copy.wait()
        if prefetch_next:
            for copy in weight_copies(j + 1, 1 - slot):
                copy.start()
        h = h_ref[...]
        gate = jnp.dot(h, wg_buf[slot], preferred_element_type=F32)
        up = jnp.dot(h, wu_buf[slot], preferred_element_type=F32)
        act = (gate / (1.0 + jnp.exp(-gate)) * up).astype(BF16)
        o_ref[...] += jnp.dot(act, wd_buf[slot], preferred_element_type=F32)

    def pair(jj, carry):
        chunk(2 * jj, 0, True)
        chunk(2 * jj + 1, 1, True)
        return carry

    n_pairs = (n_chunks - 1) // 2
    lax.fori_loop(0, n_pairs, pair, 0)
    for j in range(2 * n_pairs, n_chunks):
        chunk(j, j % 2, j + 1 < n_chunks)

    if final_norm:
        _rmsnorm_block(o_ref, gf_ref, o_ref, EPS)


def _ffn(x2d, gain, w_gate_up, w_down, layer, gain_final, final_norm):
    m, d = x2d.shape
    f = w_down.shape[1]
    bm = _block(m, 1024)
    bc = _block(f, 256)
    return pl.pallas_call(
        functools.partial(_ffn_kernel, layer=layer, n_chunks=f // bc, final_norm=final_norm),
        grid=(m // bm,),
        in_specs=[
            pl.BlockSpec(memory_space=pl.ANY),
            pl.BlockSpec((1, d), lambda i: (0, 0)),
            pl.BlockSpec(memory_space=pl.ANY),
            pl.BlockSpec(memory_space=pl.ANY),
            pl.BlockSpec((1, d), lambda i: (0, 0)),
        ],
        out_specs=pl.BlockSpec((bm, d), lambda i: (i, 0)),
        out_shape=jax.ShapeDtypeStruct((m, d), F32),
        scratch_shapes=[
            pltpu.VMEM((bm, d), BF16),
            pltpu.VMEM((2, d, bc), BF16), pltpu.VMEM((2, d, bc), BF16), pltpu.VMEM((2, bc, d), BF16),
            pltpu.SemaphoreType.DMA((2,)), pltpu.SemaphoreType.DMA((2, 3)),
        ],
        compiler_params=_params("arbitrary"),
        name="ffn",
    )(x2d, gain, w_gate_up, w_down, gain_final)


def _dft_matrices(n, scale):
    r = 1
    while r * r < n:
        r *= 2
    k = jnp.arange(n, dtype=jnp.int32)[:, None]
    hi = jnp.arange(n // r, dtype=jnp.int32)[None, :] * r
    lo = jnp.arange(r, dtype=jnp.int32)[None, :]
    w = 2.0 * math.pi / n
    a_hi = ((k * hi) % n).astype(F32) * w
    a_lo = ((k * lo) % n).astype(F32) * w
    ch, sh = jnp.cos(a_hi)[:, :, None], jnp.sin(a_hi)[:, :, None]
    cl, sl = jnp.cos(a_lo)[:, None, :], jnp.sin(a_lo)[:, None, :]
    c = (ch * cl - sh * sl).reshape(n, n) * scale
    s = (sh * cl + ch * sl).reshape(n, n) * scale
    return c.astype(BF16), s.astype(BF16)


RADIX = 4
DFT_ROWS = 128


def _twiddle_table(seq):
    n = jnp.arange(seq // RADIX, dtype=jnp.int32)[:, None]
    rho = jnp.arange(1, RADIX, dtype=jnp.int32)[None, :]
    ang = (n * rho).astype(F32) * (2.0 * math.pi / seq)
    cs = jnp.stack([jnp.cos(ang), jnp.sin(ang)], axis=-1).reshape(seq // RADIX, 2 * (RADIX - 1))
    return jnp.concatenate([cs, jnp.zeros((seq // RADIX, 2), F32)], axis=1)


def _dft_ch_kernel(x0_ref, x1_ref, x2_ref, x3_ref, g_ref, cs_ref, tw_ref, zr_ref, zi_ref,
                   sum_ref, alt_ref, d02_ref, d13_ref, *, group_dim):
    g = g_ref[...]

    def body(r, carry):
        rows = pl.ds(pl.multiple_of(r * NORM_ROWS, NORM_ROWS), NORM_ROWS)
        h0 = _rmsnorm_rows(x0_ref[rows, :], g, EPS)
        h1 = _rmsnorm_rows(x1_ref[rows, :], g, EPS)
        h2 = _rmsnorm_rows(x2_ref[rows, :], g, EPS)
        h3 = _rmsnorm_rows(x3_ref[rows, :], g, EPS)
        even, odd = h0 + h2, h1 + h3
        sum_ref[rows, :] = (even + odd).astype(BF16)
        alt_ref[rows, :] = (even - odd).astype(BF16)
        d02_ref[rows, :] = (h0 - h2).astype(BF16)
        d13_ref[rows, :] = (h1 - h3).astype(BF16)
        return carry

    lax.fori_loop(0, x0_ref.shape[0] // NORM_ROWS, body, 0)

    cs = cs_ref[...]
    tw = tw_ref[...]
    gd = group_dim
    for grp in range(sum_ref.shape[1] // gd):
        cols = slice(grp * gd, (grp + 1) * gd)
        p = jnp.dot(sum_ref[:, cols], cs, preferred_element_type=F32)
        m = jnp.dot(alt_ref[:, cols], cs, preferred_element_type=F32)
        c = jnp.dot(d02_ref[:, cols], cs, preferred_element_type=F32)
        d = jnp.dot(d13_ref[:, cols], cs, preferred_element_type=F32)
        u = [(p[:, :gd], -p[:, gd:]),
             (c[:, :gd] - d[:, gd:], -(c[:, gd:] + d[:, :gd])),
             (m[:, :gd], -m[:, gd:]),
             (c[:, :gd] + d[:, gd:], d[:, :gd] - c[:, gd:])]
        for rho, (ur, ui) in enumerate(u):
            if rho:
                ct, st = tw[:, 2 * rho - 2:2 * rho - 1], tw[:, 2 * rho - 1:2 * rho]
                ur, ui = ur * ct + ui * st, ui * ct - ur * st
            zr_ref[rho, :, cols] = ur.astype(zr_ref.dtype)
            zi_ref[rho, :, cols] = ui.astype(zi_ref.dtype)


def _dft_channels(x2d, gain, cs, tw, batch, seq):
    m, d = x2d.shape
    gd = cs.shape[0]
    quarter = seq // RADIX
    bm = _block(quarter, DFT_ROWS)
    ni = quarter // bm

    def x_spec(q):
        return pl.BlockSpec((bm, d), lambda b, i: (b * RADIX * ni + q * ni + i, 0))

    z_spec = pl.BlockSpec((RADIX, bm, d), lambda b, i: (0, b * ni + i, 0))
    return pl.pallas_call(
        functools.partial(_dft_ch_kernel, group_dim=gd),
        grid=(batch, ni),
        in_specs=[x_spec(0), x_spec(1), x_spec(2), x_spec(3),
                  pl.BlockSpec((1, d), lambda b, i: (0, 0)),
                  pl.BlockSpec(cs.shape, lambda b, i: (0, 0)),
                  pl.BlockSpec((bm, tw.shape[1]), lambda b, i: (i, 0))],
        out_specs=[z_spec, z_spec],
        out_shape=[jax.ShapeDtypeStruct((RADIX, m // RADIX, d), BF16)] * 2,
        scratch_shapes=[pltpu.VMEM((bm, d), BF16)] * 4,
        compiler_params=_params("parallel", "arbitrary"),
        name="dft_channels",
    )(x2d, x2d, x2d, x2d, gain, cs, tw)


def _dft_pos_kernel(c_ref, s_ref, zr_ref, zi_ref, o_ref, rows_ref):
    cmat, smat = c_ref[...], s_ref[...]
    quarter = cmat.shape[0]
    chunks = o_ref.shape[1] // LANES
    for rho in range(RADIX):
        y = (jnp.dot(cmat, zr_ref[rho], preferred_element_type=F32)
             + jnp.dot(smat, zi_ref[rho], preferred_element_type=F32))
        for c in range(chunks):
            rows_ref[c, pl.ds(rho, quarter, stride=RADIX), :] = y[:, c * LANES:(c + 1) * LANES]
    for c in range(chunks):
        o_ref[:, c * LANES:(c + 1) * LANES] = rows_ref[c].astype(o_ref.dtype)


def _dft_positions(cmat, smat, zr, zi, batch):
    _, mq, d = zr.shape
    quarter = cmat.shape[0]
    bn = _block(d, max(LANES, (512 * 1024) // quarter))
    mat = pl.BlockSpec((quarter, quarter), lambda b, j: (0, 0))
    col = pl.BlockSpec((RADIX, quarter, bn), lambda b, j: (0, b, j))
    return pl.pallas_call(
        _dft_pos_kernel,
        grid=(batch, d // bn),
        in_specs=[mat, mat, col, col],
        out_specs=pl.BlockSpec((RADIX * quarter, bn), lambda b, j: (b, j)),
        out_shape=jax.ShapeDtypeStruct((mq * RADIX, d), BF16),
        scratch_shapes=[pltpu.VMEM((bn // LANES, RADIX * quarter, LANES), F32)],
        compiler_params=_params("parallel", "arbitrary"),
        name="dft_positions",
    )(cmat, smat, zr, zi)


def _trunk(x, p):
    batch, seq, d = x.shape
    depth = p["norm_mix"].shape[0]
    x2d = x.reshape(batch * seq, d)
    rope = _rope_tables(seq)
    group_dim = d // N_FOURIER_GROUPS
    for i in range(depth):
        j = i // N_MIXERS
        gain_mix = p["norm_mix"][i][None, :]
        if i % N_MIXERS == 0:
            lambda_init = 0.8 - 0.6 * math.exp(-0.3 * i)
            qkv = _qkv_proj(x2d, gain_mix, p["w_qkv"][j], rope, seq)
            o = _diff_attention(qkv, p["lq1"][j][None, :], p["lk1"][j][None, :], p["lq2"][j][None, :],
                                p["lk2"][j][None, :], p["subln"][j][None, :], batch, seq, lambda_init)
            x2d = _proj_residual(o, p["w_o"][j], x2d)
        else:
            cc, sc = _dft_matrices(group_dim, group_dim ** -0.5)
            cq, sq = _dft_matrices(seq // RADIX, seq ** -0.5)
            zr, zi = _dft_channels(x2d, gain_mix, jnp.concatenate([cc, sc], axis=1), _twiddle_table(seq),
                                   batch, seq)
            f = _dft_positions(cq, sq, zr, zi, batch)
            x2d = _proj_residual(f, p["w_f"][j], x2d)
        last = i == depth - 1
        x2d = _ffn(x2d, p["norm_ffn"][i][None, :], p["w_gate_up"], p["w_down"], i,
                   p["norm_final"][None, :], final_norm=last)
    return x2d.reshape(batch, seq, d)


def kernel(x_prompt, x_sample, norm_mix, norm_ffn, norm_final, attn_w_qkv, attn_w_o, attn_lambda_q1,
           attn_lambda_k1, attn_lambda_q2, attn_lambda_k2, attn_subln, fourier_w_o, ffn_w_gate_up, ffn_w_down):
    assert x_prompt.shape[-1] % V_DIM == 0 and x_prompt.shape[-1] % (N_FOURIER_GROUPS * LANES) == 0
    p = {
        "norm_mix": norm_mix, "norm_ffn": norm_ffn, "norm_final": norm_final,
        "w_qkv": attn_w_qkv.astype(BF16), "w_o": attn_w_o.astype(BF16),
        "lq1": attn_lambda_q1, "lk1": attn_lambda_k1, "lq2": attn_lambda_q2, "lk2": attn_lambda_k2,
        "subln": attn_subln, "w_f": fourier_w_o.astype(BF16),
        "w_gate_up": ffn_w_gate_up.astype(BF16), "w_down": ffn_w_down.astype(BF16),
    }
    return (_trunk(x_prompt, p), _trunk(x_sample, p))
```

```python
import functools
import math

import numpy as np
import jax
import jax.numpy as jnp
from jax import lax
from jax.experimental import pallas as pl
from jax.experimental.pallas import tpu as pltpu

HEAD_DIM = 128
V_DIM = 2 * HEAD_DIM
ROT_DIM = HEAD_DIM // 4
ROT_HALF = ROT_DIM // 2
ROPE_THETA = 500000.0
N_FOURIER_GROUPS = 8
N_MIXERS = 2
EPS = 1e-6
SUBLN_EPS = 1e-5
LANES = 128
VMEM_LIMIT_BYTES = 58 * 1024 * 1024

BF16 = jnp.bfloat16
F32 = jnp.float32


def _block(n, want):
    b = min(n, want)
    while n % b:
        b //= 2
    return b


def _params(*sem):
    return pltpu.CompilerParams(dimension_semantics=sem, vmem_limit_bytes=VMEM_LIMIT_BYTES)


def _rmsnorm_rows(x, g, eps):
    return x * lax.rsqrt(jnp.mean(x * x, axis=-1, keepdims=True) + eps) * g


NORM_ROWS = 16
NORM_UNROLL = 4


def _rmsnorm_block(x_ref, g_ref, o_ref, eps):
    g = g_ref[...]
    step = NORM_ROWS * NORM_UNROLL

    def body(r, carry):
        base = r * step
        groups = [pl.ds(pl.multiple_of(base + u * NORM_ROWS, NORM_ROWS), NORM_ROWS) for u in range(NORM_UNROLL)]
        xs = [x_ref[rows, :] for rows in groups]
        for rows, x in zip(groups, xs):
            o_ref[rows, :] = _rmsnorm_rows(x, g, eps).astype(o_ref.dtype)
        return carry

    lax.fori_loop(0, x_ref.shape[0] // step, body, 0)


ROPE_COLS = 512


def _qkv_kernel(x_ref, g_ref, w_ref, cos_ref, sa_ref, sb_ref, o_ref, h_ref, *, n_q_blocks, q_scale):
    j = pl.program_id(1)

    @pl.when(j == 0)
    def _():
        _rmsnorm_block(x_ref, g_ref, h_ref, EPS)

    def project(lo=0, width=None):
        width = w_ref.shape[1] if width is None else width
        return jnp.dot(h_ref[...], w_ref[:, lo:lo + width], preferred_element_type=F32)

    def rotary(scale):
        cos, sa, sb = cos_ref[...] * scale, sa_ref[...] * scale, sb_ref[...] * scale
        width = min(ROPE_COLS, w_ref.shape[1])
        for lo in range(0, w_ref.shape[1], width):
            acc = project(lo, width)
            for c in range(lo, lo + width, HEAD_DIM):
                t = acc[:, c - lo:c - lo + HEAD_DIM]
                r = t * cos + pltpu.roll(t, HEAD_DIM - ROT_HALF, 1) * sa + pltpu.roll(t, ROT_HALF, 1) * sb
                o_ref[:, c:c + HEAD_DIM] = r.astype(o_ref.dtype)

    @pl.when(j < n_q_blocks)
    def _():
        rotary(q_scale)

    @pl.when(jnp.logical_and(j >= n_q_blocks, j < 2 * n_q_blocks))
    def _():
        rotary(1.0)

    @pl.when(j >= 2 * n_q_blocks)
    def _():
        o_ref[...] = project().astype(o_ref.dtype)


def _qkv_proj(x2d, gain, w_qkv, rope, seq):
    m, d = x2d.shape
    n = w_qkv.shape[1]
    bm = _block(seq, 512)
    bn = _block(d, 1024)
    n_pos_blocks = seq // bm
    cos, sa, sb = rope
    rope_spec = pl.BlockSpec((bm, HEAD_DIM), lambda i, j: (i % n_pos_blocks, 0))
    return pl.pallas_call(
        functools.partial(_qkv_kernel, n_q_blocks=d // bn, q_scale=HEAD_DIM ** -0.5 * math.log2(math.e)),
        grid=(m // bm, n // bn),
        in_specs=[
            pl.BlockSpec((bm, d), lambda i, j: (i, 0)),
            pl.BlockSpec((1, d), lambda i, j: (0, 0)),
            pl.BlockSpec((d, bn), lambda i, j: (0, j)),
            rope_spec, rope_spec, rope_spec,
        ],
        out_specs=pl.BlockSpec((bm, bn), lambda i, j: (i, j)),
        out_shape=jax.ShapeDtypeStruct((m, n), BF16),
        scratch_shapes=[pltpu.VMEM((bm, d), BF16)],
        compiler_params=_params("parallel", "arbitrary"),
        name="qkv",
    )(x2d, gain, w_qkv, cos, sa, sb)


def _rope_tables(seq):
    inv_freq = ROPE_THETA ** (-jnp.arange(0, ROT_DIM, 2, dtype=F32) / ROT_DIM)
    ang = jnp.arange(seq, dtype=F32)[:, None] * inv_freq[None, :]
    c, s = jnp.cos(ang), jnp.sin(ang)
    pad = HEAD_DIM - ROT_DIM
    cos = jnp.concatenate([c, c, jnp.ones((seq, pad), F32)], axis=1)
    sa = jnp.concatenate([-s, jnp.zeros((seq, pad + ROT_HALF), F32)], axis=1)
    sb = jnp.concatenate([jnp.zeros((seq, ROT_HALF), F32), s, jnp.zeros((seq, pad), F32)], axis=1)
    return cos, sa, sb


ATTN_ROWS = 128


def _row_groups(n_rows):
    rows = min(ATTN_ROWS, n_rows)
    return [(r0, rows) for r0 in range(0, n_rows, rows)]


def _attn_kernel(q_ref, k_ref, v_ref, lq1_ref, lk1_ref, lq2_ref, lk2_ref, sub_ref, o_ref, *, lambda_init):
    lam = (jnp.exp(jnp.sum(lq1_ref[...] * lk1_ref[...], axis=-1, keepdims=True))
           - jnp.exp(jnp.sum(lq2_ref[...] * lk2_ref[...], axis=-1, keepdims=True))
           + lambda_init)
    k = k_ref[...]
    v = v_ref[...]
    nt = (((1,), (1,)), ((), ()))
    tasks = [(r0, rows, c) for r0, rows in _row_groups(q_ref.shape[0]) for c in range(2)]

    def scores(t):
        r0, rows, c = tasks[t]
        return lax.dot_general(q_ref[r0:r0 + rows, c * HEAD_DIM:(c + 1) * HEAD_DIM],
                               k[:, c * HEAD_DIM:(c + 1) * HEAD_DIM], nt, preferred_element_type=F32)

    def exponentials(s):
        p = jnp.exp2(s - jnp.max(s, axis=-1, keepdims=True))
        return p.astype(BF16), jnp.sum(p, axis=-1, keepdims=True)

    n = len(tasks)
    s = {t: scores(t) for t in range(min(2, n))}
    p = {0: exponentials(s.pop(0))}
    pv, norm = {}, {}
    for t in range(n):
        pb, norm[t] = p.pop(t)
        pv[t] = jnp.dot(pb, v, preferred_element_type=F32)
        if t + 2 < n:
            s[t + 2] = scores(t + 2)
        if t + 1 < n:
            p[t + 1] = exponentials(s.pop(t + 1))
        if tasks[t][2] == 1:
            r0, rows, _ = tasks[t]
            o = pv.pop(t - 1) * (1.0 / norm[t - 1]) - pv.pop(t) * (lam / norm[t])
            o = _rmsnorm_rows(o, sub_ref[...], SUBLN_EPS) * (1.0 - lambda_init)
            o_ref[r0:r0 + rows, :] = o.astype(o_ref.dtype)


def _diff_attention(qkv, lq1, lk1, lq2, lk2, subln, batch, seq, lambda_init):
    m, n3 = qkv.shape
    d = n3 // 3
    heads = d // V_DIM
    bq = _block(seq, 1024)
    nq = seq // bq
    vec = pl.BlockSpec((1, HEAD_DIM), lambda b, h, i: (0, 0))
    return pl.pallas_call(
        functools.partial(_attn_kernel, lambda_init=lambda_init),
        grid=(batch, heads, nq),
        in_specs=[
            pl.BlockSpec((bq, V_DIM), lambda b, h, i: (b * nq + i, h)),
            pl.BlockSpec((seq, V_DIM), lambda b, h, i: (b, heads + h)),
            pl.BlockSpec((seq, V_DIM), lambda b, h, i: (b, 2 * heads + h)),
            vec, vec, vec, vec,
            pl.BlockSpec((1, V_DIM), lambda b, h, i: (0, 0)),
        ],
        out_specs=pl.BlockSpec((bq, V_DIM), lambda b, h, i: (b * nq + i, h)),
        out_shape=jax.ShapeDtypeStruct((m, d), BF16),
        compiler_params=_params("parallel", "parallel", "arbitrary"),
        name="diff_attn",
    )(qkv, qkv, qkv, lq1, lk1, lq2, lk2, subln)


def _proj_kernel(a_ref, w_ref, res_ref, o_ref):
    o_ref[...] = res_ref[...] + jnp.dot(a_ref[...], w_ref[...], preferred_element_type=F32)


def _proj_residual(a, w, res):
    m, k = a.shape
    n = w.shape[1]
    bm = _block(m, 1024)
    bn = _block(n, 1024)
    return pl.pallas_call(
        _proj_kernel,
        grid=(m // bm, n // bn),
        in_specs=[
            pl.BlockSpec((bm, k), lambda i, j: (i, 0)),
            pl.BlockSpec((k, bn), lambda i, j: (0, j)),
            pl.BlockSpec((bm, bn), lambda i, j: (i, j)),
        ],
        out_specs=pl.BlockSpec((bm, bn), lambda i, j: (i, j)),
        out_shape=jax.ShapeDtypeStruct((m, n), F32),
        compiler_params=_params("parallel", "arbitrary"),
        name="proj_residual",
    )(a, w, res)


def _ffn_kernel(x_hbm, g_ref, wgu_hbm, wd_hbm, gf_ref, o_ref, h_ref, wg_buf, wu_buf, wd_buf, x_sem, w_sem, *,
                layer, n_chunks, final_norm):
    bm = o_ref.shape[0]
    bc = wg_buf.shape[2]

    def weight_copies(j, slot):
        lo = pl.multiple_of(j * bc, bc)
        return (
            pltpu.make_async_copy(wgu_hbm.at[layer, :, pl.ds(lo, bc)], wg_buf.at[slot], w_sem.at[slot, 0]),
            pltpu.make_async_copy(wgu_hbm.at[layer, :, pl.ds(pl.multiple_of(n_chunks * bc + lo, bc), bc)],
                                  wu_buf.at[slot], w_sem.at[slot, 1]),
            pltpu.make_async_copy(wd_hbm.at[layer, pl.ds(lo, bc), :], wd_buf.at[slot], w_sem.at[slot, 2]),
        )

    for copy in weight_copies(0, 0):
        copy.start()
    half = bm // 2
    fetches = []
    for part in range(2):
        src = x_hbm.at[pl.ds(pl.multiple_of(pl.program_id(0) * bm + part * half, half), half), :]
        fetches.append(pltpu.make_async_copy(src, o_ref.at[pl.ds(part * half, half), :], x_sem.at[part]))
        fetches[part].start()
    for part in range(2):
        fetches[part].wait()
        part_rows = pl.ds(part * half, half)
        _rmsnorm_block(o_ref.at[part_rows, :], g_ref, h_ref.at[part_rows, :], EPS)

    def chunk(j, slot, prefetch_next):
        for copy in weight_copies(j, slot):
            copy.wait()
        if prefetch_next:
            for copy in weight_copies(j + 1, 1 - slot):
                copy.start()
        h = h_ref[...]
        gate = jnp.dot(h, wg_buf[slot], preferred_element_type=F32)
        up = jnp.dot(h, wu_buf[slot], preferred_element_type=F32)
        act = (gate / (1.0 + jnp.exp(-gate)) * up).astype(BF16)
        o_ref[...] += jnp.dot(act, wd_buf[slot], preferred_element_type=F32)

    def pair(jj, carry):
        chunk(2 * jj, 0, True)
        chunk(2 * jj + 1, 1, True)
        return carry

    n_pairs = (n_chunks - 1) // 2
    lax.fori_loop(0, n_pairs, pair, 0)
    for j in range(2 * n_pairs, n_chunks):
        chunk(j, j % 2, j + 1 < n_chunks)

    if final_norm:
        _rmsnorm_block(o_ref, gf_ref, o_ref, EPS)


def _ffn(x2d, gain, w_gate_up, w_down, layer, gain_final, final_norm):
    m, d = x2d.shape
    f = w_down.shape[1]
    bm = _block(m, 1024)
    bc = _block(f, 256)
    return pl.pallas_call(
        functools.partial(_ffn_kernel, layer=layer, n_chunks=f // bc, final_norm=final_norm),
        grid=(m // bm,),
        in_specs=[
            pl.BlockSpec(memory_space=pl.ANY),
            pl.BlockSpec((1, d), lambda i: (0, 0)),
            pl.BlockSpec(memory_space=pl.ANY),
            pl.BlockSpec(memory_space=pl.ANY),
            pl.BlockSpec((1, d), lambda i: (0, 0)),
        ],
        out_specs=pl.BlockSpec((bm, d), lambda i: (i, 0)),
        out_shape=jax.ShapeDtypeStruct((m, d), F32),
        scratch_shapes=[
            pltpu.VMEM((bm, d), BF16),
            pltpu.VMEM((2, d, bc), BF16), pltpu.VMEM((2, d, bc), BF16), pltpu.VMEM((2, bc, d), BF16),
            pltpu.SemaphoreType.DMA((2,)), pltpu.SemaphoreType.DMA((2, 3)),
        ],
        compiler_params=_params("arbitrary"),
        name="ffn",
    )(x2d, gain, w_gate_up, w_down, gain_final)


def _dft_matrices(n, scale):
    r = 1
    while r * r < n:
        r *= 2
    k = jnp.arange(n, dtype=jnp.int32)[:, None]
    hi = jnp.arange(n // r, dtype=jnp.int32)[None, :] * r
    lo = jnp.arange(r, dtype=jnp.int32)[None, :]
    w = 2.0 * math.pi / n
    a_hi = ((k * hi) % n).astype(F32) * w
    a_lo = ((k * lo) % n).astype(F32) * w
    ch, sh = jnp.cos(a_hi)[:, :, None], jnp.sin(a_hi)[:, :, None]
    cl, sl = jnp.cos(a_lo)[:, None, :], jnp.sin(a_lo)[:, None, :]
    c = (ch * cl - sh * sl).reshape(n, n) * scale
    s = (sh * cl + ch * sl).reshape(n, n) * scale
    return c.astype(BF16), s.astype(BF16)


RADIX = 4
DFT_ROWS = 128


def _twiddle_table(seq):
    n = jnp.arange(seq // RADIX, dtype=jnp.int32)[:, None]
    rho = jnp.arange(1, RADIX, dtype=jnp.int32)[None, :]
    ang = (n * rho).astype(F32) * (2.0 * math.pi / seq)
    cs = jnp.stack([jnp.cos(ang), jnp.sin(ang)], axis=-1).reshape(seq // RADIX, 2 * (RADIX - 1))
    return jnp.concatenate([cs, jnp.zeros((seq // RADIX, 2), F32)], axis=1)


def _dft_ch_kernel(x0_ref, x1_ref, x2_ref, x3_ref, g_ref, cs_ref, tw_ref, zr_ref, zi_ref,
                   sum_ref, alt_ref, d02_ref, d13_ref, *, group_dim):
    g = g_ref[...]

    def body(r, carry):
        rows = pl.ds(pl.multiple_of(r * NORM_ROWS, NORM_ROWS), NORM_ROWS)
        h0 = _rmsnorm_rows(x0_ref[rows, :], g, EPS)
        h1 = _rmsnorm_rows(x1_ref[rows, :], g, EPS)
        h2 = _rmsnorm_rows(x2_ref[rows, :], g, EPS)
        h3 = _rmsnorm_rows(x3_ref[rows, :], g, EPS)
        even, odd = h0 + h2, h1 + h3
        sum_ref[rows, :] = (even + odd).astype(BF16)
        alt_ref[rows, :] = (even - odd).astype(BF16)
        d02_ref[rows, :] = (h0 - h2).astype(BF16)
        d13_ref[rows, :] = (h1 - h3).astype(BF16)
        return carry

    lax.fori_loop(0, x0_ref.shape[0] // NORM_ROWS, body, 0)

    cs = cs_ref[...]
    tw = tw_ref[...]
    gd = group_dim
    for grp in range(sum_ref.shape[1] // gd):
        cols = slice(grp * gd, (grp + 1) * gd)
        p = jnp.dot(sum_ref[:, cols], cs, preferred_element_type=F32)
        m = jnp.dot(alt_ref[:, cols], cs, preferred_element_type=F32)
        c = jnp.dot(d02_ref[:, cols], cs, preferred_element_type=F32)
        d = jnp.dot(d13_ref[:, cols], cs, preferred_element_type=F32)
        u = [(p[:, :gd], -p[:, gd:]),
             (c[:, :gd] - d[:, gd:], -(c[:, gd:] + d[:, :gd])),
             (m[:, :gd], -m[:, gd:]),
             (c[:, :gd] + d[:, gd:], d[:, :gd] - c[:, gd:])]
        for rho, (ur, ui) in enumerate(u):
            if rho:
                ct, st = tw[:, 2 * rho - 2:2 * rho - 1], tw[:, 2 * rho - 1:2 * rho]
                ur, ui = ur * ct + ui * st, ui * ct - ur * st
            zr_ref[rho, :, cols] = ur.astype(zr_ref.dtype)
            zi_ref[rho, :, cols] = ui.astype(zi_ref.dtype)


def _dft_channels(x2d, gain, cs, tw, batch, seq):
    m, d = x2d.shape
    gd = cs.shape[0]
    quarter = seq // RADIX
    bm = _block(quarter, DFT_ROWS)
    ni = quarter // bm

    def x_spec(q):
        return pl.BlockSpec((bm, d), lambda b, i: (b * RADIX * ni + q * ni + i, 0))

    z_spec = pl.BlockSpec((RADIX, bm, d), lambda b, i: (0, b * ni + i, 0))
    return pl.pallas_call(
        functools.partial(_dft_ch_kernel, group_dim=gd),
        grid=(batch, ni),
        in_specs=[x_spec(0), x_spec(1), x_spec(2), x_spec(3),
                  pl.BlockSpec((1, d), lambda b, i: (0, 0)),
                  pl.BlockSpec(cs.shape, lambda b, i: (0, 0)),
                  pl.BlockSpec((bm, tw.shape[1]), lambda b, i: (i, 0))],
        out_specs=[z_spec, z_spec],
        out_shape=[jax.ShapeDtypeStruct((RADIX, m // RADIX, d), BF16)] * 2,
        scratch_shapes=[pltpu.VMEM((bm, d), BF16)] * 4,
        compiler_params=_params("parallel", "arbitrary"),
        name="dft_channels",
    )(x2d, x2d, x2d, x2d, gain, cs, tw)


def _dft_pos_kernel(c_ref, s_ref, zr_ref, zi_ref, o_ref, rows_ref):
    cmat, smat = c_ref[...], s_ref[...]
    quarter = cmat.shape[0]
    chunks = o_ref.shape[1] // LANES
    for rho in range(RADIX):
        y = (jnp.dot(cmat, zr_ref[rho], preferred_element_type=F32)
             + jnp.dot(smat, zi_ref[rho], preferred_element_type=F32))
        for c in range(chunks):
            rows_ref[c, pl.ds(rho, quarter, stride=RADIX), :] = y[:, c * LANES:(c + 1) * LANES]
    for c in range(chunks):
        o_ref[:, c * LANES:(c + 1) * LANES] = rows_ref[c].astype(o_ref.dtype)


def _dft_positions(cmat, smat, zr, zi, batch):
    _, mq, d = zr.shape
    quarter = cmat.shape[0]
    bn = _block(d, max(LANES, (512 * 1024) // quarter))
    mat = pl.BlockSpec((quarter, quarter), lambda b, j: (0, 0))
    col = pl.BlockSpec((RADIX, quarter, bn), lambda b, j: (0, b, j))
    return pl.pallas_call(
        _dft_pos_kernel,
        grid=(batch, d // bn),
        in_specs=[mat, mat, col, col],
        out_specs=pl.BlockSpec((RADIX * quarter, bn), lambda b, j: (b, j)),
        out_shape=jax.ShapeDtypeStruct((mq * RADIX, d), BF16),
        scratch_shapes=[pltpu.VMEM((bn // LANES, RADIX * quarter, LANES), F32)],
        compiler_params=_params("parallel", "arbitrary"),
        name="dft_positions",
    )(cmat, smat, zr, zi)


def _trunk(x, p):
    batch, seq, d = x.shape
    depth = p["norm_mix"].shape[0]
    x2d = x.reshape(batch * seq, d)
    rope = _rope_tables(seq)
    group_dim = d // N_FOURIER_GROUPS
    for i in range(depth):
        j = i // N_MIXERS
        gain_mix = p["norm_mix"][i][None, :]
        if i % N_MIXERS == 0:
            lambda_init = 0.8 - 0.6 * math.exp(-0.3 * i)
            qkv = _qkv_proj(x2d, gain_mix, p["w_qkv"][j], rope, seq)
            o = _diff_attention(qkv, p["lq1"][j][None, :], p["lk1"][j][None, :], p["lq2"][j][None, :],
                                p["lk2"][j][None, :], p["subln"][j][None, :], batch, seq, lambda_init)
            x2d = _proj_residual(o, p["w_o"][j], x2d)
        else:
            cc, sc = _dft_matrices(group_dim, group_dim ** -0.5)
            cq, sq = _dft_matrices(seq // RADIX, seq ** -0.5)
            zr, zi = _dft_channels(x2d, gain_mix, jnp.concatenate([cc, sc], axis=1), _twiddle_table(seq),
                                   batch, seq)
            f = _dft_positions(cq, sq, zr, zi, batch)
            x2d = _proj_residual(f, p["w_f"][j], x2d)
        last = i == depth - 1
        x2d = _ffn(x2d, p["norm_ffn"][i][None, :], p["w_gate_up"], p["w_down"], i,
                   p["norm_final"][None, :], final_norm=last)
    return x2d.reshape(batch, seq, d)


def kernel(x_prompt, x_sample, norm_mix, norm_ffn, norm_final, attn_w_qkv, attn_w_o, attn_lambda_q1,
           attn_lambda_k1, attn_lambda_q2, attn_lambda_k2, attn_subln, fourier_w_o, ffn_w_gate_up, ffn_w_down):
    assert x_prompt.shape[-1] % V_DIM == 0 and x_prompt.shape[-1] % (N_FOURIER_GROUPS * LANES) == 0
    p = {
        "norm_mix": norm_mix, "norm_ffn": norm_ffn, "norm_final": norm_final,
        "w_qkv": attn_w_qkv.astype(BF16), "w_o": attn_w_o.astype(BF16),
        "lq1": attn_lambda_q1, "lk1": attn_lambda_k1, "lq2": attn_lambda_q2, "lk2": attn_lambda_k2,
        "subln": attn_subln, "w_f": fourier_w_o.astype(BF16),
        "w_gate_up": ffn_w_gate_up.astype(BF16), "w_down": ffn_w_down.astype(BF16),
    }
    return (_trunk(x_prompt, p), _trunk(x_sample, p))
```
